```python
import jax
import jax.numpy as jnp
from jax import lax
import numpy as np

D_MODEL = 1024
BATCH = 4
SEQ = 4096
DEPTH = 4

GRID_W = 64
CTX_LEN = 256
EPS = 1e-6
NEG_INF = -1e30
D_A = 1024
K_A = 3
N_HEADS = 8
N_KV = 2
HEAD_DIM = 128
GROUP = N_HEADS // N_KV
Q_W = N_HEADS * HEAD_DIM
KV_W = N_KV * HEAD_DIM
WINDOW = 128
BLOCK = 128
ROPE_THETA = 10000.0
D_C = 1024
K_C = 4
LRU_BLOCKS = 8
LRU_BS = D_C // LRU_BLOCKS
LRU_C = 8.0
D_D = 1024
K_D = 31
N_BRANCH = 4
D_FF = -(-8 * D_MODEL // (3 * 256)) * 256
SPLIT_SIZES = (D_A, D_A, D_A, Q_W, KV_W, KV_W, D_C, D_C, D_D, D_D, N_BRANCH * D_MODEL)
N_IN = 3 * D_A + Q_W + 2 * KV_W + 2 * D_C + 2 * D_D + N_BRANCH * D_MODEL

kernel_name = 'hybrid_parallel_dit_block'


def _rmsnorm(x, g):
    x32 = x.astype(jnp.float32)
    y = x32 * lax.rsqrt(jnp.mean(jnp.square(x32), axis=-1, keepdims=True) + EPS)
    return (y * g.astype(jnp.float32)).astype(x.dtype)


def _layernorm(x, g, b):
    x32 = x.astype(jnp.float32)
    mu = jnp.mean(x32, axis=-1, keepdims=True)
    var = jnp.mean(jnp.square(x32 - mu), axis=-1, keepdims=True)
    y = (x32 - mu) * lax.rsqrt(var + EPS) * g.astype(jnp.float32) + b.astype(jnp.float32)
    return y.astype(x.dtype)


def _modulate(x, g, shift, scale):
    return _rmsnorm(x, g) * (1.0 + scale) + shift


def _dwconv(x, w):
    k = w.shape[0]
    left = k // 2
    return lax.conv_general_dilated(
        x, w[:, None, :].astype(x.dtype), window_strides=(1,),
        padding=[(left, k - 1 - left)],
        dimension_numbers=('NWC', 'WIO', 'NWC'),
        feature_group_count=x.shape[-1])


def _split_cols(p):
    cuts, acc = [], 0
    for s in SPLIT_SIZES[:-1]:
        acc += s
        cuts.append(acc)
    return jnp.split(p, cuts, axis=-1)


def _heads(t, n):
    return t.reshape(t.shape[0], t.shape[1], n, HEAD_DIM)


def _axial_rope(n_tok):
    rows = n_tok // GRID_W
    row = jnp.repeat(jnp.arange(rows), GRID_W).astype(jnp.float32)
    col = jnp.tile(jnp.arange(GRID_W), rows).astype(jnp.float32)
    half = HEAD_DIM // 2
    inv_freq = ROPE_THETA ** (-jnp.arange(0, half, 2, dtype=jnp.float32) / half)
    ang_r = row[:, None] * inv_freq[None, :]
    ang_c = col[:, None] * inv_freq[None, :]
    ang = jnp.concatenate([ang_r, ang_r, ang_c, ang_c], axis=-1)
    return jnp.cos(ang), jnp.sin(ang)


def _rotate(t, cos, sin):
    t1, t2, t3, t4 = jnp.split(t, 4, axis=-1)
    t_rot = jnp.concatenate([-t2, t1, -t4, t3], axis=-1)
    return (t * cos[:, None, :] + t_rot * sin[:, None, :]).astype(t.dtype)


def _band(t, nb):
    tb = t.reshape(t.shape[0], nb, BLOCK, t.shape[2], t.shape[3])
    tp = jnp.pad(tb, ((0, 0), (1, 1), (0, 0), (0, 0), (0, 0)))
    return jnp.concatenate([tp[:, :-2], tp[:, 1:-1], tp[:, 2:]], axis=2)


def _window_attention(q, k, v, kc, vc, sink):
    B, L = q.shape[0], q.shape[1]
    nb = L // BLOCK
    n_loc = 3 * BLOCK
    n_ctx = kc.shape[1]
    scale = HEAD_DIM ** -0.5
    qb = q.reshape(B, nb, BLOCK, N_KV, GROUP, HEAD_DIM)
    kb, vb = _band(k, nb), _band(v, nb)
    s_loc = jnp.einsum('bnqkgd,bnskd->bnkgqs', qb, kb, preferred_element_type=jnp.float32) * scale
    blk = jnp.arange(nb)[:, None]
    qpos = blk * BLOCK + jnp.arange(BLOCK)[None, :]
    kpos = (blk - 1) * BLOCK + jnp.arange(n_loc)[None, :]
    valid = ((jnp.abs(qpos[:, :, None] - kpos[:, None, :]) <= WINDOW)
             & (kpos[:, None, :] >= 0) & (kpos[:, None, :] < L))
    s_loc = jnp.where(valid[None, :, None, None], s_loc, NEG_INF)
    s_ctx = jnp.einsum('bnqkgd,bskd->bnkgqs', qb, kc, preferred_element_type=jnp.float32) * scale
    s_sink = jnp.broadcast_to(sink.astype(jnp.float32).reshape(1, 1, N_KV, GROUP, 1, 1), s_loc.shape[:-1] + (1,))
    p = jax.nn.softmax(jnp.concatenate([s_loc, s_ctx, s_sink], axis=-1), axis=-1).astype(v.dtype)
    o = (jnp.einsum('bnkgqs,bnskd->bnqkgd', p[..., :n_loc], vb)
         + jnp.einsum('bnkgqs,bskd->bnqkgd', p[..., n_loc:n_loc + n_ctx], vc))
    return o.reshape(B, L, Q_W)


def _context_attention(q, k, v, sink):
    B, Lc = q.shape[0], q.shape[1]
    qg = q.reshape(B, Lc, N_KV, GROUP, HEAD_DIM)
    s = jnp.einsum('bqkgd,bskd->bkgqs', qg, k, preferred_element_type=jnp.float32) * (HEAD_DIM ** -0.5)
    s_sink = jnp.broadcast_to(sink.astype(jnp.float32).reshape(1, N_KV, GROUP, 1, 1), s.shape[:-1] + (1,))
    p = jax.nn.softmax(jnp.concatenate([s, s_sink], axis=-1), axis=-1)[..., :Lc].astype(v.dtype)
    o = jnp.einsum('bkgqs,bskd->bqkgd', p, v)
    return o.reshape(B, Lc, Q_W)


def _linear_scan(a, b, reverse):
    def combine(e1, e2):
        a1, b1 = e1
        a2, b2 = e2
        return a1 * a2, a2 * b1 + b2
    return lax.associative_scan(combine, (a, b), axis=1, reverse=reverse)


def _rglru_gates(u, wa, ba, wx, bx, lam):
    B, L, C = u.shape
    ub = u.reshape(B, L, LRU_BLOCKS, LRU_BS)
    r = jax.nn.sigmoid(jnp.einsum('blnc,ncd->blnd', ub, wa, preferred_element_type=jnp.float32).reshape(B, L, C)
                       + ba.astype(jnp.float32))
    i = jax.nn.sigmoid(jnp.einsum('blnc,ncd->blnd', ub, wx, preferred_element_type=jnp.float32).reshape(B, L, C)
                       + bx.astype(jnp.float32))
    log_a = -LRU_C * jax.nn.softplus(-lam.astype(jnp.float32)) * r
    a = jnp.exp(log_a)
    b = jnp.sqrt(-jnp.expm1(2.0 * log_a)) * (i * u.astype(jnp.float32))
    return a, b


def _rglru_direction(uc, u, wa, ba, wx, bx, lam, reverse):
    ac, bc = _rglru_gates(uc, wa, ba, wx, bx, lam)
    _, hc = _linear_scan(ac, bc, reverse)
    h0 = hc[:, 0] if reverse else hc[:, -1]
    a, b = _rglru_gates(u, wa, ba, wx, bx, lam)
    a_cum, h = _linear_scan(a, b, reverse)
    return hc, h + a_cum * h0[:, None, :]


def _rglru_bidir(uc, u, wa, ba, wx, bx, lam):
    hc_f, h_f = _rglru_direction(uc, u, wa[0], ba[0], wx[0], bx[0], lam[0], False)
    hc_b, h_b = _rglru_direction(uc, u, wa[1], ba[1], wx[1], bx[1], lam[1], True)
    return (hc_f + hc_b).astype(u.dtype), (h_f + h_b).astype(u.dtype)


def _short_conv(b_gate, c_gate, u, w):
    return b_gate * _dwconv(c_gate * u, w)


def _conformer_conv(u_val, u_gate, w, bias, g, beta):
    z = _dwconv(u_val * jax.nn.sigmoid(u_gate), w) + bias
    return jax.nn.silu(_layernorm(z, g, beta))


def _merge(gate_logits, ys, w_out):
    B, L = gate_logits.shape[0], gate_logits.shape[1]
    g = jax.nn.sigmoid(gate_logits).reshape(B, L, N_BRANCH, D_MODEL)
    y = jnp.stack(ys, axis=2)
    return jnp.sum(g * y, axis=2) @ w_out


def _swiglu(h, w_in, w_out):
    gate, up = jnp.split(h @ w_in, 2, axis=-1)
    return (jax.nn.silu(gate) * up) @ w_out


def _mix_sublayer(h, hc, cos, sin, need_ctx, w_in, conv_a, sink, conv_c, conv_c_b, lru_wa, lru_ba,
                  lru_wx, lru_bx, lru_lam, conv_d, conv_d_b, ln_d_g, ln_d_b,
                  w_a_out, w_b_out, w_c_out, w_d_out, w_out):
    a_b, a_c, a_u, q, k, v, r_g, r_u, d_v, d_g, gl = _split_cols(h @ w_in)
    ca_b, ca_c, ca_u, cq, ck, cv, cr_g, cr_u, cd_v, cd_g, cgl = _split_cols(hc @ w_in)
    kc, vc = _heads(ck, N_KV), _heads(cv, N_KV)
    y_b = _window_attention(_rotate(_heads(q, N_HEADS), cos, sin), _rotate(_heads(k, N_KV), cos, sin),
                            _heads(v, N_KV), kc, vc, sink)
    u = _dwconv(r_u, conv_c) + conv_c_b
    uc = _dwconv(cr_u, conv_c) + conv_c_b
    rec_c, rec = _rglru_bidir(uc, u, lru_wa, lru_ba, lru_wx, lru_bx, lru_lam)
    ys = (_short_conv(a_b, a_c, a_u, conv_a) @ w_a_out,
          y_b @ w_b_out,
          (jax.nn.gelu(r_g) * rec) @ w_c_out,
          _conformer_conv(d_v, d_g, conv_d, conv_d_b, ln_d_g, ln_d_b) @ w_d_out)
    out = _merge(gl, ys, w_out)
    if not need_ctx:
        return out, None
    ys_c = (_short_conv(ca_b, ca_c, ca_u, conv_a) @ w_a_out,
            _context_attention(_heads(cq, N_HEADS), kc, vc, sink) @ w_b_out,
            (jax.nn.gelu(cr_g) * rec_c) @ w_c_out,
            _conformer_conv(cd_v, cd_g, conv_d, conv_d_b, ln_d_g, ln_d_b) @ w_d_out)
    return out, _merge(cgl, ys_c, w_out)


def setup_inputs(seed: int = 0) -> dict:
    key = jax.random.key(seed)
    ks = iter(jax.random.split(key, 40))
    f32 = jnp.float32

    def nrm(shape, scale):
        return jax.random.normal(next(ks), shape, f32) * scale

    u = jax.random.uniform(next(ks), (DEPTH, 2, D_C), f32, 0.9, 0.999)
    s = u ** (1.0 / LRU_C)
    lam = jnp.log(s) - jnp.log1p(-s)
    return {
        'x': nrm((BATCH, SEQ, D_MODEL), 1.0),
        'c': nrm((BATCH, D_MODEL), 1.0),
        'ctx': nrm((BATCH, CTX_LEN, D_MODEL), 1.0),
        'c_ctx': nrm((D_MODEL,), 1.0),
        'w_mod': nrm((DEPTH, D_MODEL, 6 * D_MODEL), 0.5 * D_MODEL ** -0.5),
        'b_mod': nrm((DEPTH, 6 * D_MODEL), 0.02),
        'g_mix': 1.0 + nrm((DEPTH, D_MODEL), 0.02),
        'w_in': nrm((DEPTH, D_MODEL, N_IN), D_MODEL ** -0.5),
        'conv_a': nrm((DEPTH, K_A, D_A), K_A ** -0.5),
        'sink': nrm((DEPTH, N_HEADS), 0.5),
        'conv_c': nrm((DEPTH, K_C, D_C), K_C ** -0.5),
        'conv_c_b': nrm((DEPTH, D_C), 0.02),
        'lru_wa': nrm((DEPTH, 2, LRU_BLOCKS, LRU_BS, LRU_BS), LRU_BS ** -0.5),
        'lru_ba': nrm((DEPTH, 2, D_C), 0.02),
        'lru_wx': nrm((DEPTH, 2, LRU_BLOCKS, LRU_BS, LRU_BS), LRU_BS ** -0.5),
        'lru_bx': nrm((DEPTH, 2, D_C), 0.02),
        'lru_lam': lam,
        'conv_d': nrm((DEPTH, K_D, D_D), K_D ** -0.5),
        'conv_d_b': nrm((DEPTH, D_D), 0.02),
        'ln_d_g': 1.0 + nrm((DEPTH, D_D), 0.02),
        'ln_d_b': nrm((DEPTH, D_D), 0.02),
        'w_a_out': nrm((DEPTH, D_A, D_MODEL), D_A ** -0.5),
        'w_b_out': nrm((DEPTH, Q_W, D_MODEL), Q_W ** -0.5),
        'w_c_out': nrm((DEPTH, D_C, D_MODEL), D_C ** -0.5),
        'w_d_out': nrm((DEPTH, D_D, D_MODEL), D_D ** -0.5),
        'w_out': nrm((DEPTH, D_MODEL, D_MODEL), D_MODEL ** -0.5),
        'g_ffn': 1.0 + nrm((DEPTH, D_MODEL), 0.02),
        'w_ffn_in': nrm((DEPTH, D_MODEL, 2 * D_FF), D_MODEL ** -0.5),
        'w_ffn_out': nrm((DEPTH, D_FF, D_MODEL), D_FF ** -0.5),
        'g_final': 1.0 + nrm((D_MODEL,), 0.02),
    }


def reference(x, c, ctx, c_ctx, w_mod, b_mod, g_mix, w_in, conv_a, sink, conv_c, conv_c_b, lru_wa, lru_ba,
              lru_wx, lru_bx, lru_lam, conv_d, conv_d_b, ln_d_g, ln_d_b, w_a_out, w_b_out, w_c_out, w_d_out,
              w_out, g_ffn, w_ffn_in, w_ffn_out, g_final):
    cos, sin = _axial_rope(x.shape[1])
    xc = ctx
    for l in range(DEPTH):
        need_ctx = l < DEPTH - 1
        m = jax.nn.silu(c) @ w_mod[l] + b_mod[l]
        mc = jax.nn.silu(c_ctx) @ w_mod[l] + b_mod[l]
        sh1, sc1, gt1, sh2, sc2, gt2 = jnp.split(m[:, None, :], 6, axis=-1)
        csh1, csc1, cgt1, csh2, csc2, cgt2 = jnp.split(mc[None, None, :], 6, axis=-1)
        o, oc = _mix_sublayer(
            _modulate(x, g_mix[l], sh1, sc1), _modulate(xc, g_mix[l], csh1, csc1), cos, sin, need_ctx,
            w_in[l], conv_a[l], sink[l], conv_c[l], conv_c_b[l], lru_wa[l], lru_ba[l], lru_wx[l], lru_bx[l],
            lru_lam[l], conv_d[l], conv_d_b[l], ln_d_g[l], ln_d_b[l], w_a_out[l], w_b_out[l], w_c_out[l],
            w_d_out[l], w_out[l])
        x = x + gt1 * o
        x = x + gt2 * _swiglu(_modulate(x, g_ffn[l], sh2, sc2), w_ffn_in[l], w_ffn_out[l])
        if need_ctx:
            xc = xc + cgt1 * oc
            xc = xc + cgt2 * _swiglu(_modulate(xc, g_ffn[l], csh2, csc2), w_ffn_in[l], w_ffn_out[l])
    return _rmsnorm(x, g_final)
```

```python
import functools

import jax
import jax.numpy as jnp
from jax import lax
from jax.experimental import pallas as pl
from jax.experimental.pallas import tpu as pltpu

F32 = jnp.float32
BF16 = jnp.bfloat16

D = 1024
DEPTH = 4
GRID_W = 64
EPS = 1e-6
NEG_INF = -1e30
K_A = 3
N_HEADS = 8
N_KV = 2
HEAD_DIM = 128
GROUP = N_HEADS // N_KV
WINDOW = 128
ROPE_THETA = 10000.0
K_C = 4
LRU_BLOCKS = 8
LRU_BS = D // LRU_BLOCKS
LRU_C = 8.0
K_D = 31
D_FF = 2816
N_IN = 12800

C_AB, C_AC, C_AU, C_RG, C_RU, C_DV, C_DG, C_GL, C_Q = 0, 1, 2, 3, 4, 5, 6, 7, 11
C_K, C_V = 48, 49
HALO = 16
SUBLANES = 8
LANES = 128
N_CHUNK = D // LANES
SLAB_PAD = 8
VMEM_LIMIT = 56 * 1024 * 1024


def _cparams(sem):
    return pltpu.CompilerParams(dimension_semantics=sem, vmem_limit_bytes=VMEM_LIMIT)


def _resident(shape):
    nd = len(shape)
    return pl.BlockSpec(shape, lambda *_: (0,) * nd, pipeline_mode=pl.Buffered(1))


def _modnorm(x, g, shift, scale):
    y = x * lax.rsqrt(jnp.mean(x * x, axis=-1, keepdims=True) + EPS)
    return (y * g) * (1.0 + scale) + shift


def _mod_kernel(c_ref, w_ref, b_ref, o_ref):
    s = c_ref[...]
    s = s * jax.nn.sigmoid(s)
    w = w_ref[...]
    s_hi = s.astype(BF16)
    s_lo = (s - s_hi.astype(F32)).astype(BF16)
    w_hi = w.astype(BF16)
    w_lo = (w - w_hi.astype(F32)).astype(BF16)
    acc = jnp.dot(s_hi, w_hi, preferred_element_type=F32)
    acc += jnp.dot(s_lo, w_hi, preferred_element_type=F32)
    acc += jnp.dot(s_hi, w_lo, preferred_element_type=F32)
    o_ref[...] = acc + b_ref[...]


def _mod_call(cc, w_mod, b_mod):
    tn = 1536
    return pl.pallas_call(
        _mod_kernel,
        grid=(DEPTH, 6 * D // tn),
        in_specs=[
            pl.BlockSpec((8, D), lambda l, j: (0, 0)),
            pl.BlockSpec((None, D, tn), lambda l, j: (l, 0, j)),
            pl.BlockSpec((None, 1, tn), lambda l, j: (l, 0, j)),
        ],
        out_specs=pl.BlockSpec((None, 8, tn), lambda l, j: (l, 0, j)),
        out_shape=jax.ShapeDtypeStruct((DEPTH, 8, 6 * D), F32),
        compiler_params=_cparams(("arbitrary", "arbitrary")),
    )(cc, w_mod, b_mod.reshape(DEPTH, 1, 6 * D))


IN_TN = 512
J_Q0 = C_Q * D // IN_TN
J_KV = J_Q0 + D // IN_TN
Q_SCALE = HEAD_DIM ** -0.5


def _rope(t, tab_ref):
    cos = tab_ref[:, 0:LANES]
    sin_lo = tab_ref[:, LANES:2 * LANES]
    sin_hi = tab_ref[:, 2 * LANES:3 * LANES]
    return t * cos + pltpu.roll(t, 96, 1) * sin_lo + pltpu.roll(t, 32, 1) * sin_hi


def _inproj_kernel(*refs, rope):
    if rope:
        x_ref, mod_ref, g_ref, tab_ref, w_ref, o_ref, h_ref = refs
    else:
        x_ref, mod_ref, g_ref, w_ref, o_ref, h_ref = refs
        tab_ref = None
    j = pl.program_id(2)

    @pl.when(j == 0)
    def _():
        h_ref[...] = _modnorm(x_ref[...], g_ref[...], mod_ref[0:1, :], mod_ref[1:2, :]).astype(BF16)

    acc = jnp.dot(h_ref[...], w_ref[...], preferred_element_type=F32)
    n_chunk = IN_TN // LANES

    def chunk(c):
        return acc[:, c * LANES:(c + 1) * LANES]

    @pl.when(j < J_Q0)
    def _():
        o_ref[...] = acc.astype(BF16)

    @pl.when((j >= J_Q0) & (j < J_KV))
    def _():
        for c in range(n_chunk):
            t = chunk(c)
            if rope:
                t = _rope(t, tab_ref)
            o_ref[:, c * LANES:(c + 1) * LANES] = (t * Q_SCALE).astype(BF16)

    @pl.when(j == J_KV)
    def _():
        for c in range(n_chunk):
            t = chunk(c)
            if rope and c < n_chunk // 2:
                t = _rope(t, tab_ref)
            o_ref[:, c * LANES:(c + 1) * LANES] = t.astype(BF16)


def _inproj_call(x, mods_l, mod_row, g, tab, w, tm):
    bx, lx, _ = x.shape
    rope = tab is not None
    in_specs = [
        pl.BlockSpec((None, tm, D), lambda b, i, j: (b, i, 0)),
        pl.BlockSpec((None, 6, D), lambda b, i, j: (mod_row(b), 0, 0)),
        pl.BlockSpec((1, D), lambda b, i, j: (0, 0)),
    ]
    args = [x, mods_l, g]
    if rope:
        in_specs.append(pl.BlockSpec((tm, 3 * LANES), lambda b, i, j: (i, 0)))
        args.append(tab)
    in_specs.append(pl.BlockSpec((D, IN_TN), lambda b, i, j: (0, j)))
    args.append(w)
    return pl.pallas_call(
        functools.partial(_inproj_kernel, rope=rope),
        grid=(bx, lx // tm, N_IN // IN_TN),
        in_specs=in_specs,
        out_specs=pl.BlockSpec((None, tm, IN_TN), lambda b, i, j: (b, i, j)),
        out_shape=jax.ShapeDtypeStruct((bx, lx, N_IN), BF16),
        scratch_shapes=[pltpu.VMEM((tm, D), BF16)],
        compiler_params=_cparams(("arbitrary", "arbitrary", "arbitrary")),
    )(*args)


def _pcol(t, c, width=D):
    return pl.BlockSpec((None, t, width), lambda b, i: (b, i, c))


def _halo_specs(t, n_t, c, tile_of=lambda i: i):
    per = t // HALO
    last = n_t * per - 1
    return [
        pl.BlockSpec((None, HALO, D), lambda b, i: (b, jnp.maximum(tile_of(i) * per - 1, 0), c)),
        pl.BlockSpec((None, t, D), lambda b, i: (b, tile_of(i), c)),
        pl.BlockSpec((None, HALO, D), lambda b, i: (b, jnp.minimum((tile_of(i) + 1) * per, last), c)),
    ]


def _edge_masks(tile, n_t):
    has_prev = jnp.where(tile > 0, 1.0, 0.0).astype(F32)
    has_next = jnp.where(tile < n_t - 1, 1.0, 0.0).astype(F32)
    return has_prev, has_next


def _tile_rows(j, pitch):
    return (pl.ds(j, N_CHUNK, stride=pitch), slice(None))


def _fill_ext_slab(slab, t, prev, cur, nxt):
    pitch = t + 2 * HALO + SLAB_PAD
    for c in range(N_CHUNK):
        cs = slice(c * LANES, (c + 1) * LANES)
        slab[c * pitch:c * pitch + HALO, :] = prev(cs)
        slab[c * pitch + HALO:c * pitch + HALO + t, :] = cur(cs)
        slab[c * pitch + HALO + t:c * pitch + 2 * HALO + t, :] = nxt(cs)


def _fold(slab, pitch, dst, n_rows):
    def body(r, carry):
        dst[r] = slab[_tile_rows(r, pitch)]
        return carry

    lax.fori_loop(0, n_rows, body, 0, unroll=8)


def _conv_folded(ext_f, w_ref, k_taps, t, emit, rows_per_iter=2):
    off = HALO - k_taps // 2

    def body(it, carry):
        t0 = it * rows_per_iter
        loaded = {}

        def x(r):
            if r not in loaded:
                loaded[r] = ext_f[t0 + (off + r)]
            return loaded[r]

        accs = [None] * rows_per_iter
        for k in range(k_taps):
            w = w_ref[k]
            for r in range(rows_per_iter):
                term = w * x(k + r)
                accs[r] = term if accs[r] is None else accs[r] + term
        for r in range(rows_per_iter):
            emit(t0 + r, accs[r])
        return carry

    lax.fori_loop(0, t // rows_per_iter, body, 0, unroll=2)


def _lru_kernel(pfp, pfc, pfn, pbp, pbc, pbn, cw_ref, cb_ref, wab_ref, bab_ref, lam_ref, h0_ref,
                hf_ref, hb_ref, hl_ref, slab_e, ext_f, slab_u, af, bf, ab, bb, carry, *, t, n_t):
    i = pl.program_id(1)
    pitch_e = t + 2 * HALO + SLAB_PAD
    pitch = t + SLAB_PAD

    @pl.when(i == 0)
    def _():
        carry[...] = h0_ref[...]

    for d, (prv, cur, nxt, a_slab, b_slab) in enumerate(((pfp, pfc, pfn, af, bf), (pbp, pbc, pbn, ab, bb))):
        tile = i if d == 0 else n_t - 1 - i
        has_prev, has_next = _edge_masks(tile, n_t)
        _fill_ext_slab(slab_e, t,
                       lambda cs: prv[:, cs].astype(F32) * has_prev,
                       lambda cs: cur[:, cs].astype(F32),
                       lambda cs: nxt[:, cs].astype(F32) * has_next)
        _fold(slab_e, pitch_e, ext_f, t + 2 * HALO)

        bias = cb_ref[...]

        def emit(row, tile_val):
            slab_u[_tile_rows(row, pitch)] = tile_val + bias

        _conv_folded(ext_f, cw_ref, K_C, t, emit)

        neg_lam = -lam_ref[d:d + 1, :]
        softplus = jnp.maximum(neg_lam, 0.0) + jnp.log1p(jnp.exp(-jnp.abs(neg_lam)))
        c8 = -LRU_C * softplus
        for n in range(LRU_BLOCKS):
            cs = slice(n * LRU_BS, (n + 1) * LRU_BS)
            rs = slice(n * pitch, n * pitch + t)
            u = slab_u[rs, :]
            g = jnp.dot(u.astype(BF16), wab_ref[d, n], preferred_element_type=F32)
            r = jax.nn.sigmoid(g[:, :LRU_BS] + bab_ref[d:d + 1, cs])
            ig = jax.nn.sigmoid(g[:, LRU_BS:] + bab_ref[2 + d:3 + d, cs])
            log_a = c8[:, cs] * r
            a = jnp.exp(log_a)
            gain = jnp.sqrt(-jnp.tanh(log_a) * (1.0 + a * a))
            a_slab[rs, :] = a
            b_slab[rs, :] = gain * (ig * u)

    def step(j, c):
        h_f, h_b = c
        jb = t - 1 - j
        rf = _tile_rows(j, pitch)
        rb = _tile_rows(jb, pitch)
        h_f = af[rf] * h_f + bf[rf]
        h_b = ab[rb] * h_b + bb[rb]
        bf[rf] = h_f
        bb[rb] = h_b
        return h_f, h_b

    h_f, h_b = lax.fori_loop(0, t, step, (carry[0], carry[1]), unroll=8)
    carry[0] = h_f
    carry[1] = h_b
    hl_ref[...] = carry[...]
    for n in range(N_CHUNK):
        cs = slice(n * LANES, (n + 1) * LANES)
        rs = slice(n * pitch, n * pitch + t)
        hf_ref[:, cs] = bf[rs, :].astype(BF16)
        hb_ref[:, cs] = bb[rs, :].astype(BF16)


def _lru_call(p, h0, conv_c, conv_c_b, wab, bab, lam, t):
    b, l, _ = p.shape
    n_t = l // t
    rev = lambda i: n_t - 1 - i
    small = lambda shape: pl.BlockSpec(shape, lambda b, i: (0,) * len(shape))
    state = lambda: pl.BlockSpec((None, 2, SUBLANES, LANES), lambda b, i: (b, 0, 0, 0))
    slab = lambda rows: pltpu.VMEM((N_CHUNK * (rows + SLAB_PAD), LANES), F32)
    return pl.pallas_call(
        functools.partial(_lru_kernel, t=t, n_t=n_t),
        grid=(b, n_t),
        in_specs=(_halo_specs(t, n_t, C_RU) + _halo_specs(t, n_t, C_RU, rev) + [
            small((K_C, SUBLANES, LANES)), small((SUBLANES, LANES)),
            small((2, LRU_BLOCKS, LRU_BS, 2 * LRU_BS)), small((4, D)), small((2, D)), state()]),
        out_specs=[
            pl.BlockSpec((None, t, D), lambda b, i: (b, i, 0)),
            pl.BlockSpec((None, t, D), lambda b, i: (b, rev(i), 0)),
            state(),
        ],
        out_shape=[
            jax.ShapeDtypeStruct((b, l, D), BF16),
            jax.ShapeDtypeStruct((b, l, D), BF16),
            jax.ShapeDtypeStruct((b, 2, SUBLANES, LANES), F32),
        ],
        scratch_shapes=[
            slab(t + 2 * HALO), pltpu.VMEM((t + 2 * HALO, SUBLANES, LANES), F32), slab(t),
            slab(t), slab(t), slab(t), slab(t), pltpu.VMEM((2, SUBLANES, LANES), F32),
        ],
        compiler_params=_cparams(("arbitrary", "arbitrary")),
    )(p, p, p, p, p, p, conv_c.reshape(K_C, SUBLANES, LANES), conv_c_b.reshape(SUBLANES, LANES), wab, bab, lam, h0)


def _sink_column(sink_ref, kh, rows_per_head):
    rid = lax.broadcasted_iota(jnp.int32, (GROUP * rows_per_head, 1), 0)
    col = sink_ref[kh * GROUP + GROUP - 1:kh * GROUP + GROUP, 0:1]
    for g in range(GROUP - 2, -1, -1):
        col = jnp.where(rid < (g + 1) * rows_per_head, sink_ref[kh * GROUP + g:kh * GROUP + g + 1, 0:1], col)
    return col


def _qk(q, k):
    return lax.dot_general(q, k, (((1,), (1,)), ((), ())), preferred_element_type=F32)


def _attn_kernel(q_ref, k_ref, v_ref, kc_ref, vc_ref, sink_ref, o_ref, *, tq, l):
    i = pl.program_id(1)
    blk = WINDOW
    n_loc = 3 * blk
    nqb = tq // blk
    rows = GROUP * blk
    qrow = lax.broadcasted_iota(jnp.int32, (rows, n_loc), 0) & (blk - 1)
    kcol = lax.broadcasted_iota(jnp.int32, (rows, n_loc), 1)
    rel0 = qrow - kcol
    for qb in range(nqb):
        n = i * nqb + qb
        start = pl.multiple_of(jnp.clip((n - 1) * blk, 0, l - n_loc), blk)
        valid = jnp.abs(rel0 + (n * blk - start)) <= WINDOW
        rs = slice(qb * blk, (qb + 1) * blk)
        for kh in range(N_KV):
            ks = slice(kh * HEAD_DIM, (kh + 1) * HEAD_DIM)
            q4 = jnp.concatenate(
                [q_ref[rs, (kh * GROUP + g) * HEAD_DIM:(kh * GROUP + g + 1) * HEAD_DIM] for g in range(GROUP)],
                axis=0)
            s_loc = jnp.where(valid, _qk(q4, k_ref[pl.ds(start, n_loc), ks]), NEG_INF)
            s_ctx = _qk(q4, kc_ref[:, ks])
            s_sink = _sink_column(sink_ref, kh, blk)
            m = jnp.maximum(jnp.maximum(jnp.max(s_loc, axis=-1, keepdims=True),
                                        jnp.max(s_ctx, axis=-1, keepdims=True)), s_sink)
            p_loc = jnp.exp(s_loc - m)
            p_ctx = jnp.exp(s_ctx - m)
            den = (jnp.sum(p_loc, axis=-1, keepdims=True) + jnp.sum(p_ctx, axis=-1, keepdims=True)
                   + jnp.exp(s_sink - m))
            o = jnp.dot(p_loc.astype(BF16), v_ref[pl.ds(start, n_loc), ks], preferred_element_type=F32)
            o += jnp.dot(p_ctx.astype(BF16), vc_ref[:, ks], preferred_element_type=F32)
            o = o / den
            for g in range(GROUP):
                hs = slice((kh * GROUP + g) * HEAD_DIM, (kh * GROUP + g + 1) * HEAD_DIM)
                o_ref[rs, hs] = o[g * blk:(g + 1) * blk, :].astype(BF16)


def _attn_call(p, pc, sink_b, tq):
    b, l, _ = p.shape
    lc = pc.shape[1]
    kvw = N_KV * HEAD_DIM
    return pl.pallas_call(
        functools.partial(_attn_kernel, tq=tq, l=l),
        grid=(b, l // tq),
        in_specs=[
            _pcol(tq, C_Q),
            pl.BlockSpec((None, l, kvw), lambda b, i: (b, 0, C_K)),
            pl.BlockSpec((None, l, kvw), lambda b, i: (b, 0, C_V)),
            pl.BlockSpec((None, lc, kvw), lambda b, i: (b, 0, C_K)),
            pl.BlockSpec((None, lc, kvw), lambda b, i: (b, 0, C_V)),
            pl.BlockSpec((N_HEADS, LANES), lambda b, i: (0, 0)),
        ],
        out_specs=pl.BlockSpec((None, tq, D), lambda b, i: (b, i, 0)),
        out_shape=jax.ShapeDtypeStruct((b, l, D), BF16),
        compiler_params=_cparams(("arbitrary", "arbitrary")),
    )(p, p, p, pc, pc, sink_b)


def _cattn_kernel(q_ref, kc_ref, vc_ref, sink_ref, o_ref, *, lc):
    for kh in range(N_KV):
        ks = slice(kh * HEAD_DIM, (kh + 1) * HEAD_DIM)
        q4 = jnp.concatenate(
            [q_ref[:, (kh * GROUP + g) * HEAD_DIM:(kh * GROUP + g + 1) * HEAD_DIM] for g in range(GROUP)], axis=0)
        s = _qk(q4, kc_ref[:, ks])
        s_sink = _sink_column(sink_ref, kh, lc)
        m = jnp.maximum(jnp.max(s, axis=-1, keepdims=True), s_sink)
        p = jnp.exp(s - m)
        den = jnp.sum(p, axis=-1, keepdims=True) + jnp.exp(s_sink - m)
        o = jnp.dot(p.astype(BF16), vc_ref[:, ks], preferred_element_type=F32) / den
        for g in range(GROUP):
            hs = slice((kh * GROUP + g) * HEAD_DIM, (kh * GROUP + g + 1) * HEAD_DIM)
            o_ref[:, hs] = o[g * lc:(g + 1) * lc, :].astype(BF16)


def _cattn_call(pc, sink_b):
    b, lc, _ = pc.shape
    kvw = N_KV * HEAD_DIM
    return pl.pallas_call(
        functools.partial(_cattn_kernel, lc=lc),
        grid=(b,),
        in_specs=[
            pl.BlockSpec((None, lc, D), lambda b: (b, 0, C_Q)),
            pl.BlockSpec((None, lc, kvw), lambda b: (b, 0, C_K)),
            pl.BlockSpec((None, lc, kvw), lambda b: (b, 0, C_V)),
            pl.BlockSpec((N_HEADS, LANES), lambda b: (0, 0)),
        ],
        out_specs=pl.BlockSpec((None, lc, D), lambda b: (b, 0, 0)),
        out_shape=jax.ShapeDtypeStruct((b, lc, D), BF16),
        compiler_params=_cparams(("arbitrary",)),
    )(pc, pc, pc, sink_b)


def _bout_kernel(ab_ref, acp, acc_, acn, aup, auc, aun, rg_ref, dvp, dvc, dvn, dgp, dgc, dgn,
                 ga_ref, gb_ref, gc_ref, gd_ref, hf_ref, hb_ref, yb_ref, x_ref, mod_ref,
                 ca_ref, cd_ref, cdb_ref, lng_ref, lnb_ref, wa_ref, wb_ref, wc_ref, wd_ref, wo_ref,
                 o_ref, slab_e, ext_f, slab_z, z_scr, *, t, n_t):
    i = pl.program_id(1)
    has_prev, has_next = _edge_masks(i, n_t)
    pitch_e = t + 2 * HALO + SLAB_PAD
    pitch = t + SLAB_PAD

    def f32(ref):
        return ref[...].astype(F32)

    def conv(k_taps, w_ref, prev, cur, nxt):
        _fill_ext_slab(slab_e, t, lambda cs: prev(cs) * has_prev, cur, lambda cs: nxt(cs) * has_next)
        _fold(slab_e, pitch_e, ext_f, t + 2 * HALO)

        def emit(row, tile_val):
            slab_z[_tile_rows(row, pitch)] = tile_val

        _conv_folded(ext_f, w_ref, k_taps, t, emit)
        for c in range(N_CHUNK):
            z_scr[:, c * LANES:(c + 1) * LANES] = slab_z[c * pitch:c * pitch + t, :]

    def gated(gate_ref, pre, w_ref):
        return jax.nn.sigmoid(f32(gate_ref)) * jnp.dot(pre.astype(BF16), w_ref[...], preferred_element_type=F32)

    def prod(x_ref, y_ref):
        return lambda cs: x_ref[:, cs].astype(F32) * y_ref[:, cs].astype(F32)

    def glu(v_ref, g_ref):
        return lambda cs: v_ref[:, cs].astype(F32) * jax.nn.sigmoid(g_ref[:, cs].astype(F32))

    conv(K_A, ca_ref, prod(acp, aup), prod(acc_, auc), prod(acn, aun))
    merged = gated(ga_ref, f32(ab_ref) * z_scr[...], wa_ref)
    merged += gated(gb_ref, yb_ref[...], wb_ref)
    merged += gated(gc_ref, jax.nn.gelu(f32(rg_ref)) * (f32(hf_ref) + f32(hb_ref)), wc_ref)
    conv(K_D, cd_ref, glu(dvp, dgp), glu(dvc, dgc), glu(dvn, dgn))
    z = z_scr[...] + cdb_ref[...]
    mu = jnp.mean(z, axis=-1, keepdims=True)
    zc = z - mu
    var = jnp.mean(zc * zc, axis=-1, keepdims=True)
    y = zc * lax.rsqrt(var + EPS) * lng_ref[...] + lnb_ref[...]
    merged += gated(gd_ref, y * jax.nn.sigmoid(y), wd_ref)

    o = jnp.dot(merged.astype(BF16), wo_ref[...], preferred_element_type=F32)
    o_ref[...] = x_ref[...] + mod_ref[2:3, :] * o


def _bout_call(p, hf, hb, yb, x, mods_l, mod_row, conv_a, conv_d, conv_d_b, ln_g, ln_b, wa, wb, wc, wd, wo, t):
    b, l, _ = p.shape
    n_t = l // t
    row = lambda: pl.BlockSpec((None, t, D), lambda b, i: (b, i, 0))
    in_specs = ([_pcol(t, C_AB)] + _halo_specs(t, n_t, C_AC) + _halo_specs(t, n_t, C_AU) + [_pcol(t, C_RG)]
                + _halo_specs(t, n_t, C_DV) + _halo_specs(t, n_t, C_DG)
                + [_pcol(t, C_GL + k) for k in range(4)]
                + [row(), row(), row(), row(),
                   pl.BlockSpec((None, 6, D), lambda b, i: (mod_row(b), 0, 0)),
                   _resident((K_A, SUBLANES, LANES)), _resident((K_D, SUBLANES, LANES)),
                   _resident((1, D)), _resident((1, D)), _resident((1, D))]
                + [_resident((D, D)) for _ in range(5)])
    slab = lambda rows: pltpu.VMEM((N_CHUNK * (rows + SLAB_PAD), LANES), F32)
    return pl.pallas_call(
        functools.partial(_bout_kernel, t=t, n_t=n_t),
        grid=(b, n_t),
        in_specs=in_specs,
        out_specs=pl.BlockSpec((None, t, D), lambda b, i: (b, i, 0)),
        out_shape=jax.ShapeDtypeStruct((b, l, D), F32),
        scratch_shapes=[slab(t + 2 * HALO), pltpu.VMEM((t + 2 * HALO, SUBLANES, LANES), F32), slab(t),
                        pltpu.VMEM((t, D), F32)],
        compiler_params=_cparams(("arbitrary", "arbitrary")),
    )(*([p] * 18), hf, hb, yb, x, mods_l, conv_a.reshape(K_A, SUBLANES, LANES),
      conv_d.reshape(K_D, SUBLANES, LANES), conv_d_b, ln_g, ln_b, wa, wb, wc, wd, wo)


FF_CHUNK = 256


def _ffn_kernel(x_ref, mod_ref, g_ref, wi_ref, wo_ref, gf_ref, o_ref, acc, *, final):
    x = x_ref[...]
    h = _modnorm(x, g_ref[...], mod_ref[3:4, :], mod_ref[4:5, :]).astype(BF16)
    for c in range(D_FF // FF_CHUNK):
        gate = jnp.dot(h, wi_ref[:, c * FF_CHUNK:(c + 1) * FF_CHUNK], preferred_element_type=F32)
        up = jnp.dot(h, wi_ref[:, D_FF + c * FF_CHUNK:D_FF + (c + 1) * FF_CHUNK], preferred_element_type=F32)
        act = (gate * jax.nn.sigmoid(gate) * up).astype(BF16)
        part = jnp.dot(act, wo_ref[c * FF_CHUNK:(c + 1) * FF_CHUNK, :], preferred_element_type=F32)
        if c == 0:
            acc[...] = part
        else:
            acc[...] += part
    y = x + mod_ref[5:6, :] * acc[...]
    if final:
        y = y * lax.rsqrt(jnp.mean(y * y, axis=-1, keepdims=True) + EPS) * gf_ref[...]
    o_ref[...] = y


def _ffn_call(x, mods_l, mod_row, g, wi, wo, g_final, final, t):
    b, l, _ = x.shape
    return pl.pallas_call(
        functools.partial(_ffn_kernel, final=final),
        grid=(b, l // t),
        in_specs=[
            pl.BlockSpec((None, t, D), lambda b, i: (b, i, 0)),
            pl.BlockSpec((None, 6, D), lambda b, i: (mod_row(b), 0, 0)),
            _resident((1, D)), _resident((D, 2 * D_FF)), _resident((D_FF, D)), _resident((1, D)),
        ],
        out_specs=pl.BlockSpec((None, t, D), lambda b, i: (b, i, 0)),
        out_shape=jax.ShapeDtypeStruct((b, l, D), F32),
        scratch_shapes=[pltpu.VMEM((t, D), F32)],
        compiler_params=_cparams(("arbitrary", "arbitrary")),
    )(x, mods_l, g, wi, wo, g_final)


def _rope_table(n_tok):
    rows = n_tok // GRID_W
    row = jnp.repeat(jnp.arange(rows), GRID_W).astype(F32)
    col = jnp.tile(jnp.arange(GRID_W), rows).astype(F32)
    half = HEAD_DIM // 2
    inv_freq = ROPE_THETA ** (-jnp.arange(0, half, 2, dtype=F32) / half)
    ang_r = row[:, None] * inv_freq[None, :]
    ang_c = col[:, None] * inv_freq[None, :]
    ang = jnp.concatenate([ang_r, ang_r, ang_c, ang_c], axis=-1)
    cos, sin = jnp.cos(ang), jnp.sin(ang)
    first = (jnp.arange(HEAD_DIM) % half) < half // 2
    return jnp.concatenate([cos, jnp.where(first, -sin, 0.0), jnp.where(first, 0.0, sin)], axis=-1)


def kernel(x, c, ctx, c_ctx, w_mod, b_mod, g_mix, w_in, conv_a, sink, conv_c, conv_c_b, lru_wa, lru_ba, lru_wx,
           lru_bx, lru_lam, conv_d, conv_d_b, ln_d_g, ln_d_b, w_a_out, w_b_out, w_c_out, w_d_out, w_out, g_ffn,
           w_ffn_in, w_ffn_out, g_final):
    b, l, _ = x.shape
    lc = ctx.shape[1]
    ctx_row = b
    cc = jnp.zeros((8, D), F32).at[:b].set(c).at[ctx_row].set(c_ctx)
    mods = _mod_call(cc, w_mod, b_mod).reshape(DEPTH, 8, 6, D)
    tab = _rope_table(l)
    lat_row = lambda bi: bi
    ctx_rowf = lambda bi: ctx_row
    g_fin = g_final.reshape(1, D)

    xc = ctx
    for li in range(DEPTH):
        need_ctx = li < DEPTH - 1
        w = w_in[li]
        w_p = jnp.concatenate([w[:, :3 * D], w[:, 3 * D + D + 512:], w[:, 3 * D:3 * D + D + 512]], axis=1).astype(BF16)
        wab = jnp.concatenate([lru_wa[li], lru_wx[li]], axis=-1).astype(BF16)
        bab = jnp.concatenate([lru_ba[li], lru_bx[li]], axis=0)
        sink_b = jnp.broadcast_to(sink[li][:, None], (N_HEADS, LANES))
        g1 = g_mix[li].reshape(1, D)
        g2 = g_ffn[li].reshape(1, D)
        outs = [w_a_out[li].astype(BF16), w_b_out[li].astype(BF16), w_c_out[li].astype(BF16),
                w_d_out[li].astype(BF16), w_out[li].astype(BF16)]
        bout_consts = (conv_a[li], conv_d[li], conv_d_b[li].reshape(1, D), ln_d_g[li].reshape(1, D),
                       ln_d_b[li].reshape(1, D))
        wi = w_ffn_in[li].astype(BF16)
        wo = w_ffn_out[li].astype(BF16)
        ml = mods[li]
        lru_consts = (conv_c[li], conv_c_b[li], wab, bab, lru_lam[li])

        pc = _inproj_call(xc.reshape(1, b * lc, D), ml, ctx_rowf, g1, None, w_p, b * lc).reshape(b, lc, N_IN)
        hcf, hcb, hcl = _lru_call(pc, jnp.zeros((b, 2, SUBLANES, LANES), F32), *lru_consts, lc)
        p = _inproj_call(x, ml, lat_row, g1, tab, w_p, 1024)
        hf, hb, _ = _lru_call(p, hcl, *lru_consts, 512)
        yb = _attn_call(p, pc, sink_b, 512)
        x = _bout_call(p, hf, hb, yb, x, ml, lat_row, *bout_consts, *outs, 256)
        x = _ffn_call(x, ml, lat_row, g2, wi, wo, g_fin, li == DEPTH - 1, 512)
        if need_ctx:
            ybc = _cattn_call(pc, sink_b)
            xc = _bout_call(pc, hcf, hcb, ybc, xc, ml, ctx_rowf, *bout_consts, *outs, lc)
            xc = _ffn_call(xc.reshape(1, b * lc, D), ml, ctx_rowf, g2, wi, wo, g_fin, False, 512).reshape(b, lc, D)
    return x
```

```python
import functools

import jax
import jax.numpy as jnp
from jax import lax
from jax.experimental import pallas as pl
from jax.experimental.pallas import tpu as pltpu

F32 = jnp.float32
BF16 = jnp.bfloat16

D = 1024
DEPTH = 4
GRID_W = 64
EPS = 1e-6
NEG_INF = -1e30
F32_MIN_NORMAL = 1.17549435e-38
K_A = 3
N_HEADS = 8
N_KV = 2
HEAD_DIM = 128
GROUP = N_HEADS // N_KV
WINDOW = 128
ROPE_THETA = 10000.0
K_C = 4
LRU_BLOCKS = 8
LRU_BS = D // LRU_BLOCKS
LRU_C = 8.0
K_D = 31
D_FF = 2816
N_IN = 12800

C_MA, C_AB, C_MD, C_RG, C_RU, C_GL, C_Q = 0, 2, 3, 5, 6, 7, 11
C_K, C_V = 48, 49
HALO = 16
SUBLANES = 8
LANES = 128
N_CHUNK = D // LANES
SLAB_PAD = 8
VMEM_LIMIT = 56 * 1024 * 1024


def _cparams(sem):
    return pltpu.CompilerParams(dimension_semantics=sem, vmem_limit_bytes=VMEM_LIMIT)


def _resident(shape):
    nd = len(shape)
    return pl.BlockSpec(shape, lambda *_: (0,) * nd, pipeline_mode=pl.Buffered(1))


def _modnorm(x, g, shift, scale):
    y = x * lax.rsqrt(jnp.mean(x * x, axis=-1, keepdims=True) + EPS)
    return (y * g) * (1.0 + scale) + shift


def _sigmoid(x):
    return 0.5 * jnp.tanh(0.5 * x) + 0.5


def _mod_kernel(c_ref, w_ref, b_ref, o_ref):
    s = c_ref[...]
    s = s * _sigmoid(s)
    w = w_ref[...]
    s_hi = s.astype(BF16)
    s_lo = (s - s_hi.astype(F32)).astype(BF16)
    w_hi = w.astype(BF16)
    w_lo = (w - w_hi.astype(F32)).astype(BF16)
    acc = jnp.dot(s_hi, w_hi, preferred_element_type=F32)
    acc += jnp.dot(s_lo, w_hi, preferred_element_type=F32)
    acc += jnp.dot(s_hi, w_lo, preferred_element_type=F32)
    o_ref[...] = acc + b_ref[...]


def _mod_call(cc, w_mod, b_mod):
    tn = 1536
    return pl.pallas_call(
        _mod_kernel,
        grid=(DEPTH, 6 * D // tn),
        in_specs=[
            pl.BlockSpec((8, D), lambda l, j: (0, 0)),
            pl.BlockSpec((None, D, tn), lambda l, j: (l, 0, j)),
            pl.BlockSpec((None, 1, tn), lambda l, j: (l, 0, j)),
        ],
        out_specs=pl.BlockSpec((None, 8, tn), lambda l, j: (l, 0, j)),
        out_shape=jax.ShapeDtypeStruct((DEPTH, 8, 6 * D), F32),
        compiler_params=_cparams(("arbitrary", "arbitrary")),
    )(cc, w_mod, b_mod.reshape(DEPTH, 1, 6 * D))


IN_SUB = 512
IN_GROUP = 5
IN_TN = IN_SUB * IN_GROUP
Q_SCALE = HEAD_DIM ** -0.5
IN_PERM = (2, 4, 3, 5, 0, 1, 13, 15, 14, 16, 9, 10, 11, 12, 17, 18, 19, 20, 21, 22, 23, 24, 6, 7, 8)


def _rope(t, tab_ref):
    cos = tab_ref[:, 0:LANES]
    sin_lo = tab_ref[:, LANES:2 * LANES]
    sin_hi = tab_ref[:, 2 * LANES:3 * LANES]
    return t * cos + pltpu.roll(t, 96, 1) * sin_lo + pltpu.roll(t, 32, 1) * sin_hi


def _inproj_kernel(*refs, rope):
    if rope:
        x_ref, mod_ref, g_ref, tab_ref, w_ref, o_ref, h_ref = refs
    else:
        x_ref, mod_ref, g_ref, w_ref, o_ref, h_ref = refs
        tab_ref = None
    j = pl.program_id(2)

    @pl.when(j == 0)
    def _():
        h_ref[...] = _modnorm(x_ref[...], g_ref[...], mod_ref[0:1, :], mod_ref[1:2, :]).astype(BF16)

    def mm(s):
        return jnp.dot(h_ref[...], w_ref[:, s * IN_SUB:(s + 1) * IN_SUB], preferred_element_type=F32)

    def put(pos, val):
        o_ref[:, pos * IN_SUB:(pos + 1) * IN_SUB] = val.astype(BF16)

    def zero(pos):
        o_ref[:, pos * IN_SUB:(pos + 1) * IN_SUB] = jnp.zeros((o_ref.shape[0], IN_SUB), BF16)

    def rotated(acc, n_rot, scale):
        for c in range(IN_SUB // LANES):
            t = acc[:, c * LANES:(c + 1) * LANES]
            if rope and c < n_rot:
                t = _rope(t, tab_ref)
            if scale != 1.0:
                t = t * scale
            yield c, t

    @pl.when(j == 0)
    def _():
        put(0, mm(0) * mm(1))
        put(1, mm(2) * mm(3))
        zero(2)
        zero(3)
        put(4, mm(4))

    @pl.when(j == 1)
    def _():
        put(0, mm(0))
        put(1, mm(1) * _sigmoid(mm(2)))
        put(2, mm(3) * _sigmoid(mm(4)))
        zero(3)
        zero(4)

    @pl.when(j == 2)
    def _():
        put(0, jax.nn.gelu(mm(0)))
        put(1, jax.nn.gelu(mm(1)))
        put(2, mm(2))
        put(3, mm(3))
        put(4, _sigmoid(mm(4)))

    @pl.when(j == 3)
    def _():
        for s in range(IN_GROUP):
            put(s, _sigmoid(mm(s)))

    @pl.when(j == 4)
    def _():
        put(0, _sigmoid(mm(0)))
        put(1, _sigmoid(mm(1)))
        n_chunk = IN_SUB // LANES
        for s, n_rot, scale in ((2, n_chunk, Q_SCALE), (3, n_chunk, Q_SCALE), (4, n_chunk // 2, 1.0)):
            for c, t in rotated(mm(s), n_rot, scale):
                o_ref[:, s * IN_SUB + c * LANES:s * IN_SUB + (c + 1) * LANES] = t.astype(BF16)


def _inproj_call(x, mods_l, mod_row, g, tab, w, tm):
    bx, lx, _ = x.shape
    rope = tab is not None
    in_specs = [
        pl.BlockSpec((None, tm, D), lambda b, i, j: (b, i, 0)),
        pl.BlockSpec((None, 6, D), lambda b, i, j: (mod_row(b), 0, 0)),
        pl.BlockSpec((1, D), lambda b, i, j: (0, 0)),
    ]
    args = [x, mods_l, g]
    if rope:
        in_specs.append(pl.BlockSpec((tm, 3 * LANES), lambda b, i, j: (i, 0)))
        args.append(tab)
    in_specs.append(pl.BlockSpec((D, IN_TN), lambda b, i, j: (0, j)))
    args.append(w)
    return pl.pallas_call(
        functools.partial(_inproj_kernel, rope=rope),
        grid=(bx, lx // tm, N_IN // IN_TN),
        in_specs=in_specs,
        out_specs=pl.BlockSpec((None, tm, IN_TN), lambda b, i, j: (b, i, j)),
        out_shape=jax.ShapeDtypeStruct((bx, lx, N_IN), BF16),
        scratch_shapes=[pltpu.VMEM((tm, D), BF16)],
        compiler_params=_cparams(("arbitrary", "arbitrary", "arbitrary")),
    )(*args)


def _pcol(t, c, width=D):
    return pl.BlockSpec((None, t, width), lambda b, i: (b, i, c))


def _halo_specs(t, n_t, c, tile_of=lambda i: i):
    per = t // HALO
    last = n_t * per - 1
    return [
        pl.BlockSpec((None, HALO, D), lambda b, i: (b, jnp.maximum(tile_of(i) * per - 1, 0), c)),
        pl.BlockSpec((None, t, D), lambda b, i: (b, tile_of(i), c)),
        pl.BlockSpec((None, HALO, D), lambda b, i: (b, jnp.minimum((tile_of(i) + 1) * per, last), c)),
    ]


def _edge_masks(tile, n_t):
    has_prev = jnp.where(tile > 0, 1.0, 0.0).astype(F32)
    has_next = jnp.where(tile < n_t - 1, 1.0, 0.0).astype(F32)
    return has_prev, has_next


def _tile_rows(j, pitch):
    return (pl.ds(j, N_CHUNK, stride=pitch), slice(None))


def _fill_ext_slab(slab, t, prev, cur, nxt):
    pitch = t + 2 * HALO + SLAB_PAD
    for c in range(N_CHUNK):
        cs = slice(c * LANES, (c + 1) * LANES)
        slab[c * pitch:c * pitch + HALO, :] = prev(cs)
        slab[c * pitch + HALO:c * pitch + HALO + t, :] = cur(cs)
        slab[c * pitch + HALO + t:c * pitch + 2 * HALO + t, :] = nxt(cs)


def _fold(slab, pitch, dst, n_rows):
    def body(r, carry):
        dst[r] = slab[_tile_rows(r, pitch)]
        return carry

    lax.fori_loop(0, n_rows, body, 0, unroll=8)


def _conv_folded(ext_f, w_ref, k_taps, t, emit, rows_per_iter=4):
    off = HALO - k_taps // 2

    def body(it, carry):
        t0 = it * rows_per_iter
        loaded = {}

        def x(r):
            if r not in loaded:
                loaded[r] = ext_f[t0 + (off + r)]
            return loaded[r]

        n_part = min(4, k_taps)
        parts = [[None] * n_part for _ in range(rows_per_iter)]
        for k in range(k_taps):
            w = w_ref[k]
            for r in range(rows_per_iter):
                term = w * x(k + r)
                prev = parts[r][k % n_part]
                parts[r][k % n_part] = term if prev is None else prev + term
        for r in range(rows_per_iter):
            acc = parts[r]
            while len(acc) > 1:
                acc = [acc[i] + acc[i + 1] for i in range(0, len(acc) - 1, 2)] + ([acc[-1]] if len(acc) % 2 else [])
            emit(t0 + r, acc[0])
        return carry

    lax.fori_loop(0, t // rows_per_iter, body, 0)


SCAN_GROUP = 4


def _lru_kernel(pfp, pfc, pfn, pbp, pbc, pbn, cw_ref, cb_ref, wab_ref, bab_ref, lam_ref, h0_ref,
                hf_ref, hb_ref, hl_ref, slab_e, ext_f, slab_u, af, bf, ab, bb, hb_slab, carry, *, t, n_t):
    i = pl.program_id(1)
    hf_slab = slab_u
    pitch_e = t + 2 * HALO + SLAB_PAD
    pitch = t + SLAB_PAD

    @pl.when(i == 0)
    def _():
        carry[...] = h0_ref[...]

    for d, (prv, cur, nxt, a_slab, b_slab) in enumerate(((pfp, pfc, pfn, af, bf), (pbp, pbc, pbn, ab, bb))):
        tile = i if d == 0 else n_t - 1 - i
        has_prev, has_next = _edge_masks(tile, n_t)
        _fill_ext_slab(slab_e, t,
                       lambda cs: prv[:, cs].astype(F32) * has_prev,
                       lambda cs: cur[:, cs].astype(F32),
                       lambda cs: nxt[:, cs].astype(F32) * has_next)
        _fold(slab_e, pitch_e, ext_f, t + 2 * HALO)

        bias = cb_ref[...]

        def emit(row, tile_val):
            slab_u[_tile_rows(row, pitch)] = tile_val + bias

        _conv_folded(ext_f, cw_ref, K_C, t, emit)

        neg_lam = -lam_ref[d:d + 1, :]
        softplus = jnp.maximum(neg_lam, 0.0) + jnp.log1p(jnp.exp(-jnp.abs(neg_lam)))
        c8 = -LRU_C * softplus
        for n in range(LRU_BLOCKS):
            cs = slice(n * LRU_BS, (n + 1) * LRU_BS)
            rs = slice(n * pitch, n * pitch + t)
            u = slab_u[rs, :]
            g = jnp.dot(u.astype(BF16), wab_ref[d, n], preferred_element_type=F32)
            r = _sigmoid(g[:, :LRU_BS] + bab_ref[d:d + 1, cs])
            ig = _sigmoid(g[:, LRU_BS:] + bab_ref[2 + d:3 + d, cs])
            log_a = c8[:, cs] * r
            a = jnp.exp(log_a)
            y = -jnp.tanh(log_a) * (1.0 + a * a)
            gain = y * lax.rsqrt(jnp.maximum(y, F32_MIN_NORMAL))
            a_slab[rs, :] = a
            b_slab[rs, :] = gain * (ig * u)

    def chain(a_slab, b_slab, h_slab, h, positions):
        pa = pb = out = None
        for pos in positions:
            rows = _tile_rows(pos, pitch)
            a = a_slab[rows]
            b = b_slab[rows]
            if pa is None:
                pa, pb = a, b
            else:
                pa, pb = a * pa, a * pb + b
            out = pa * h + pb
            h_slab[rows] = out
        return out

    def step(it, c):
        h_f, h_b = c
        j0 = it * SCAN_GROUP
        h_f = chain(af, bf, hf_slab, h_f, [j0 + g for g in range(SCAN_GROUP)])
        h_b = chain(ab, bb, hb_slab, h_b, [t - 1 - j0 - g for g in range(SCAN_GROUP)])
        return h_f, h_b

    h_f, h_b = lax.fori_loop(0, t // SCAN_GROUP, step, (carry[0], carry[1]), unroll=2)
    carry[0] = h_f
    carry[1] = h_b
    hl_ref[...] = carry[...]
    for n in range(N_CHUNK):
        cs = slice(n * LANES, (n + 1) * LANES)
        rs = slice(n * pitch, n * pitch + t)
        hf_ref[:, cs] = hf_slab[rs, :].astype(BF16)
        hb_ref[:, cs] = hb_slab[rs, :].astype(BF16)


def _lru_call(p, h0, conv_c, conv_c_b, wab, bab, lam, t):
    b, l, _ = p.shape
    n_t = l // t
    rev = lambda i: n_t - 1 - i
    small = lambda shape: pl.BlockSpec(shape, lambda b, i: (0,) * len(shape))
    state = lambda: pl.BlockSpec((None, 2, SUBLANES, LANES), lambda b, i: (b, 0, 0, 0))
    slab = lambda rows: pltpu.VMEM((N_CHUNK * (rows + SLAB_PAD), LANES), F32)
    return pl.pallas_call(
        functools.partial(_lru_kernel, t=t, n_t=n_t),
        grid=(b, n_t),
        in_specs=(_halo_specs(t, n_t, C_RU) + _halo_specs(t, n_t, C_RU, rev) + [
            small((K_C, SUBLANES, LANES)), small((SUBLANES, LANES)),
            small((2, LRU_BLOCKS, LRU_BS, 2 * LRU_BS)), small((4, D)), small((2, D)), state()]),
        out_specs=[
            pl.BlockSpec((None, t, D), lambda b, i: (b, i, 0)),
            pl.BlockSpec((None, t, D), lambda b, i: (b, rev(i), 0)),
            state(),
        ],
        out_shape=[
            jax.ShapeDtypeStruct((b, l, D), BF16),
            jax.ShapeDtypeStruct((b, l, D), BF16),
            jax.ShapeDtypeStruct((b, 2, SUBLANES, LANES), F32),
        ],
        scratch_shapes=[
            slab(t + 2 * HALO), pltpu.VMEM((t + 2 * HALO, SUBLANES, LANES), F32), slab(t),
            slab(t), slab(t), slab(t), slab(t), slab(t), pltpu.VMEM((2, SUBLANES, LANES), F32),
        ],
        compiler_params=_cparams(("arbitrary", "arbitrary")),
    )(p, p, p, p, p, p, conv_c.reshape(K_C, SUBLANES, LANES), conv_c_b.reshape(SUBLANES, LANES), wab, bab, lam, h0)


def _sink_column(sink_ref, kh, rows_per_head):
    rid = lax.broadcasted_iota(jnp.int32, (GROUP * rows_per_head, 1), 0)
    col = sink_ref[kh * GROUP + GROUP - 1:kh * GROUP + GROUP, 0:1]
    for g in range(GROUP - 2, -1, -1):
        col = jnp.where(rid < (g + 1) * rows_per_head, sink_ref[kh * GROUP + g:kh * GROUP + g + 1, 0:1], col)
    return col


def _qk(q, k):
    return lax.dot_general(q, k, (((1,), (1,)), ((), ())), preferred_element_type=F32)


def _stack_heads(q_ref, rs, kh):
    return jnp.concatenate(
        [q_ref[rs, (kh * GROUP + g) * HEAD_DIM:(kh * GROUP + g + 1) * HEAD_DIM] for g in range(GROUP)], axis=0)


def _attn_kernel(q_ref, k_ref, v_ref, kc_ref, vc_ref, sink_ref, o_ref, kpad, vpad, vcx, *, tq, l):
    i = pl.program_id(1)
    blk = WINDOW
    nqb = tq // blk
    nb = l // blk
    rows = GROUP * blk
    vw = 2 * HEAD_DIM

    @pl.when(i == 0)
    def _():
        kpad[0:blk, :] = jnp.zeros((blk, N_KV * HEAD_DIM), BF16)
        kpad[blk:blk + l, :] = k_ref[...]
        kpad[blk + l:2 * blk + l, :] = jnp.zeros((blk, N_KV * HEAD_DIM), BF16)
        vpad[0:blk, :] = jnp.zeros((blk, N_KV * vw), BF16)
        vpad[blk + l:2 * blk + l, :] = jnp.zeros((blk, N_KV * vw), BF16)
        for kh in range(N_KV):
            vpad[blk:blk + l, kh * vw:kh * vw + HEAD_DIM] = v_ref[:, kh * HEAD_DIM:(kh + 1) * HEAD_DIM]
            vpad[blk:blk + l, kh * vw + HEAD_DIM:(kh + 1) * vw] = jnp.ones((l, HEAD_DIM), BF16)
            vcx[:, kh * vw:kh * vw + HEAD_DIM] = vc_ref[:, kh * HEAD_DIM:(kh + 1) * HEAD_DIM]
            vcx[:, kh * vw + HEAD_DIM:(kh + 1) * vw] = jnp.ones((vc_ref.shape[0], HEAD_DIM), BF16)

    qrow = lax.broadcasted_iota(jnp.int32, (rows, blk), 0) & (blk - 1)
    kcol = lax.broadcasted_iota(jnp.int32, (rows, blk), 1)
    bias_prev = jnp.where(kcol >= qrow, 0.0, NEG_INF).astype(F32)
    bias_next = jnp.where(kcol <= qrow, 0.0, NEG_INF).astype(F32)

    units = [(qb, kh) for qb in range(nqb) for kh in range(N_KV)]

    def scores(qb, kh):
        n = i * nqb + qb
        start = pl.multiple_of(n * blk, blk)
        ks = slice(kh * HEAD_DIM, (kh + 1) * HEAD_DIM)
        q4 = _stack_heads(q_ref, slice(qb * blk, (qb + 1) * blk), kh)
        s_loc = _qk(q4, kpad[pl.ds(start, 3 * blk), ks])
        s_ctx = _qk(q4, kc_ref[:, ks])
        no_prev = jnp.where(n > 0, 0.0, NEG_INF).astype(F32)
        no_next = jnp.where(n < nb - 1, 0.0, NEG_INF).astype(F32)
        return (s_loc[:, 0:blk] + (bias_prev + no_prev), s_loc[:, blk:2 * blk],
                s_loc[:, 2 * blk:3 * blk] + (bias_next + no_next), s_ctx)

    def finish(qb, kh, s_prev, s_own, s_next, s_ctx):
        n = i * nqb + qb
        start = pl.multiple_of(n * blk, blk)
        s_sink = _sink_column(sink_ref, kh, blk)
        m_el = jnp.maximum(jnp.maximum(s_prev, s_own), s_next)
        for c in range(s_ctx.shape[1] // blk):
            m_el = jnp.maximum(m_el, s_ctx[:, c * blk:(c + 1) * blk])
        m = jnp.maximum(jnp.max(m_el, axis=-1, keepdims=True), s_sink)
        p_loc = jnp.concatenate([jnp.exp(s_prev - m).astype(BF16), jnp.exp(s_own - m).astype(BF16),
                                 jnp.exp(s_next - m).astype(BF16)], axis=1)
        p_ctx = jnp.exp(s_ctx - m).astype(BF16)
        vs = slice(kh * vw, (kh + 1) * vw)
        o = jnp.dot(p_loc, vpad[pl.ds(start, 3 * blk), vs], preferred_element_type=F32)
        o += jnp.dot(p_ctx, vcx[:, vs], preferred_element_type=F32)
        den = o[:, HEAD_DIM:vw] + jnp.exp(s_sink - m)
        o = o[:, 0:HEAD_DIM] / den
        for g in range(GROUP):
            hs = slice((kh * GROUP + g) * HEAD_DIM, (kh * GROUP + g + 1) * HEAD_DIM)
            o_ref[qb * blk:(qb + 1) * blk, hs] = o[g * blk:(g + 1) * blk, :].astype(BF16)

    pending = scores(*units[0])
    for n_u, unit in enumerate(units):
        nxt = scores(*units[n_u + 1]) if n_u + 1 < len(units) else None
        finish(*unit, *pending)
        pending = nxt


def _attn_call(p, pc, sink_b, tq):
    b, l, _ = p.shape
    lc = pc.shape[1]
    kvw = N_KV * HEAD_DIM
    return pl.pallas_call(
        functools.partial(_attn_kernel, tq=tq, l=l),
        grid=(b, l // tq),
        in_specs=[
            _pcol(tq, C_Q),
            pl.BlockSpec((None, l, kvw), lambda b, i: (b, 0, C_K)),
            pl.BlockSpec((None, l, kvw), lambda b, i: (b, 0, C_V)),
            pl.BlockSpec((None, lc, kvw), lambda b, i: (b, 0, C_K)),
            pl.BlockSpec((None, lc, kvw), lambda b, i: (b, 0, C_V)),
            pl.BlockSpec((N_HEADS, LANES), lambda b, i: (0, 0)),
        ],
        out_specs=pl.BlockSpec((None, tq, D), lambda b, i: (b, i, 0)),
        out_shape=jax.ShapeDtypeStruct((b, l, D), BF16),
        scratch_shapes=[pltpu.VMEM((l + 2 * WINDOW, kvw), BF16), pltpu.VMEM((l + 2 * WINDOW, 2 * kvw), BF16),
                        pltpu.VMEM((lc, 2 * kvw), BF16)],
        compiler_params=_cparams(("arbitrary", "arbitrary")),
    )(p, p, p, pc, pc, sink_b)


def _cattn_kernel(q_ref, kc_ref, vc_ref, sink_ref, o_ref, *, lc):
    for kh in range(N_KV):
        ks = slice(kh * HEAD_DIM, (kh + 1) * HEAD_DIM)
        q4 = _stack_heads(q_ref, slice(0, lc), kh)
        s = _qk(q4, kc_ref[:, ks])
        s_sink = _sink_column(sink_ref, kh, lc)
        m = jnp.maximum(jnp.max(s, axis=-1, keepdims=True), s_sink)
        p = jnp.exp(s - m)
        den = jnp.sum(p, axis=-1, keepdims=True) + jnp.exp(s_sink - m)
        o = jnp.dot(p.astype(BF16), vc_ref[:, ks], preferred_element_type=F32) / den
        for g in range(GROUP):
            hs = slice((kh * GROUP + g) * HEAD_DIM, (kh * GROUP + g + 1) * HEAD_DIM)
            o_ref[:, hs] = o[g * lc:(g + 1) * lc, :].astype(BF16)


def _cattn_call(pc, sink_b):
    b, lc, _ = pc.shape
    kvw = N_KV * HEAD_DIM
    return pl.pallas_call(
        functools.partial(_cattn_kernel, lc=lc),
        grid=(b,),
        in_specs=[
            pl.BlockSpec((None, lc, D), lambda b: (b, 0, C_Q)),
            pl.BlockSpec((None, lc, kvw), lambda b: (b, 0, C_K)),
            pl.BlockSpec((None, lc, kvw), lambda b: (b, 0, C_V)),
            pl.BlockSpec((N_HEADS, LANES), lambda b: (0, 0)),
        ],
        out_specs=pl.BlockSpec((None, lc, D), lambda b: (b, 0, 0)),
        out_shape=jax.ShapeDtypeStruct((b, lc, D), BF16),
        compiler_params=_cparams(("arbitrary",)),
    )(pc, pc, pc, sink_b)


def _bout_kernel(ab_ref, map_, mac, man, rg_ref, mdp, mdc, mdn, ga_ref, gb_ref, gc_ref, gd_ref,
                 hf_ref, hb_ref, yb_ref, x_ref, mod_ref,
                 ca_ref, cd_ref, cdb_ref, lng_ref, lnb_ref, wa_ref, wb_ref, wc_ref, wd_ref, wo_ref,
                 o_ref, slab_e, ext_f, slab_z, z_scr, *, t, n_t):
    i = pl.program_id(1)
    has_prev, has_next = _edge_masks(i, n_t)
    pitch_e = t + 2 * HALO + SLAB_PAD
    pitch = t + SLAB_PAD

    def f32(ref):
        return ref[...].astype(F32)

    def conv(k_taps, w_ref, prv, cur, nxt):
        _fill_ext_slab(slab_e, t,
                       lambda cs: prv[:, cs].astype(F32) * has_prev,
                       lambda cs: cur[:, cs].astype(F32),
                       lambda cs: nxt[:, cs].astype(F32) * has_next)
        _fold(slab_e, pitch_e, ext_f, t + 2 * HALO)

        def emit(row, tile_val):
            slab_z[_tile_rows(row, pitch)] = tile_val

        _conv_folded(ext_f, w_ref, k_taps, t, emit)
        for c in range(N_CHUNK):
            z_scr[:, c * LANES:(c + 1) * LANES] = slab_z[c * pitch:c * pitch + t, :]

    def gated(gate_ref, pre, w_ref):
        return f32(gate_ref) * jnp.dot(pre.astype(BF16), w_ref[...], preferred_element_type=F32)

    conv(K_A, ca_ref, map_, mac, man)
    merged = gated(ga_ref, f32(ab_ref) * z_scr[...], wa_ref)
    merged += gated(gb_ref, yb_ref[...], wb_ref)
    merged += gated(gc_ref, f32(rg_ref) * (f32(hf_ref) + f32(hb_ref)), wc_ref)
    conv(K_D, cd_ref, mdp, mdc, mdn)
    z = z_scr[...] + cdb_ref[...]
    mu = jnp.mean(z, axis=-1, keepdims=True)
    zc = z - mu
    var = jnp.mean(zc * zc, axis=-1, keepdims=True)
    y = zc * lax.rsqrt(var + EPS) * lng_ref[...] + lnb_ref[...]
    merged += gated(gd_ref, y * _sigmoid(y), wd_ref)

    o = jnp.dot(merged.astype(BF16), wo_ref[...], preferred_element_type=F32)
    o_ref[...] = x_ref[...] + mod_ref[2:3, :] * o


def _bout_call(p, hf, hb, yb, x, mods_l, mod_row, conv_a, conv_d, conv_d_b, ln_g, ln_b, wa, wb, wc, wd, wo, t):
    b, l, _ = p.shape
    n_t = l // t
    row = lambda: pl.BlockSpec((None, t, D), lambda b, i: (b, i, 0))
    p_specs = ([_pcol(t, C_AB)] + _halo_specs(t, n_t, C_MA) + [_pcol(t, C_RG)] + _halo_specs(t, n_t, C_MD)
               + [_pcol(t, C_GL + k) for k in range(4)])
    in_specs = (p_specs
                + [row(), row(), row(), row(),
                   pl.BlockSpec((None, 6, D), lambda b, i: (mod_row(b), 0, 0)),
                   _resident((K_A, SUBLANES, LANES)), _resident((K_D, SUBLANES, LANES)),
                   _resident((1, D)), _resident((1, D)), _resident((1, D))]
                + [_resident((D, D)) for _ in range(5)])
    slab = lambda rows: pltpu.VMEM((N_CHUNK * (rows + SLAB_PAD), LANES), F32)
    return pl.pallas_call(
        functools.partial(_bout_kernel, t=t, n_t=n_t),
        grid=(b, n_t),
        in_specs=in_specs,
        out_specs=pl.BlockSpec((None, t, D), lambda b, i: (b, i, 0)),
        out_shape=jax.ShapeDtypeStruct((b, l, D), F32),
        scratch_shapes=[slab(t + 2 * HALO), pltpu.VMEM((t + 2 * HALO, SUBLANES, LANES), F32), slab(t),
                        pltpu.VMEM((t, D), F32)],
        compiler_params=_cparams(("arbitrary", "arbitrary")),
    )(*([p] * len(p_specs)), hf, hb, yb, x, mods_l, conv_a.reshape(K_A, SUBLANES, LANES),
      conv_d.reshape(K_D, SUBLANES, LANES), conv_d_b, ln_g, ln_b, wa, wb, wc, wd, wo)


FF_CHUNK = 256


def _ffn_kernel(x_ref, mod_ref, g_ref, wi_ref, wo_ref, gf_ref, o_ref, acc, *, final):
    x = x_ref[...]
    h = _modnorm(x, g_ref[...], mod_ref[3:4, :], mod_ref[4:5, :]).astype(BF16)
    for c in range(D_FF // FF_CHUNK):
        gate = jnp.dot(h, wi_ref[:, c * FF_CHUNK:(c + 1) * FF_CHUNK], preferred_element_type=F32)
        up = jnp.dot(h, wi_ref[:, D_FF + c * FF_CHUNK:D_FF + (c + 1) * FF_CHUNK], preferred_element_type=F32)
        act = (gate * _sigmoid(gate) * up).astype(BF16)
        part = jnp.dot(act, wo_ref[c * FF_CHUNK:(c + 1) * FF_CHUNK, :], preferred_element_type=F32)
        if c == 0:
            acc[...] = part
        else:
            acc[...] += part
    y = x + mod_ref[5:6, :] * acc[...]
    if final:
        y = y * lax.rsqrt(jnp.mean(y * y, axis=-1, keepdims=True) + EPS) * gf_ref[...]
    o_ref[...] = y


def _ffn_call(x, mods_l, mod_row, g, wi, wo, g_final, final, t):
    b, l, _ = x.shape
    return pl.pallas_call(
        functools.partial(_ffn_kernel, final=final),
        grid=(b, l // t),
        in_specs=[
            pl.BlockSpec((None, t, D), lambda b, i: (b, i, 0)),
            pl.BlockSpec((None, 6, D), lambda b, i: (mod_row(b), 0, 0)),
            _resident((1, D)), _resident((D, 2 * D_FF)), _resident((D_FF, D)), _resident((1, D)),
        ],
        out_specs=pl.BlockSpec((None, t, D), lambda b, i: (b, i, 0)),
        out_shape=jax.ShapeDtypeStruct((b, l, D), F32),
        scratch_shapes=[pltpu.VMEM((t, D), F32)],
        compiler_params=_cparams(("arbitrary", "arbitrary")),
    )(x, mods_l, g, wi, wo, g_final)


def _rope_table(n_tok):
    rows = n_tok // GRID_W
    row = jnp.repeat(jnp.arange(rows), GRID_W).astype(F32)
    col = jnp.tile(jnp.arange(GRID_W), rows).astype(F32)
    half = HEAD_DIM // 2
    inv_freq = ROPE_THETA ** (-jnp.arange(0, half, 2, dtype=F32) / half)
    ang_r = row[:, None] * inv_freq[None, :]
    ang_c = col[:, None] * inv_freq[None, :]
    ang = jnp.concatenate([ang_r, ang_r, ang_c, ang_c], axis=-1)
    cos, sin = jnp.cos(ang), jnp.sin(ang)
    first = (jnp.arange(HEAD_DIM) % half) < half // 2
    return jnp.concatenate([cos, jnp.where(first, -sin, 0.0), jnp.where(first, 0.0, sin)], axis=-1)


def _seq_tile(l, want):
    return min(l, want)


def kernel(x, c, ctx, c_ctx, w_mod, b_mod, g_mix, w_in, conv_a, sink, conv_c, conv_c_b, lru_wa, lru_ba, lru_wx,
           lru_bx, lru_lam, conv_d, conv_d_b, ln_d_g, ln_d_b, w_a_out, w_b_out, w_c_out, w_d_out, w_out, g_ffn,
           w_ffn_in, w_ffn_out, g_final):
    b, l, _ = x.shape
    lc = ctx.shape[1]
    ctx_row = b
    cc = jnp.zeros((8, D), F32).at[:b].set(c).at[ctx_row].set(c_ctx)
    mods = _mod_call(cc, w_mod, b_mod).reshape(DEPTH, 8, 6, D)
    tab = _rope_table(l)
    lat_row = lambda bi: bi
    ctx_rowf = lambda bi: ctx_row
    g_fin = g_final.reshape(1, D)

    xc = ctx
    for li in range(DEPTH):
        need_ctx = li < DEPTH - 1
        w_p = jnp.concatenate([w_in[li][:, s * IN_SUB:(s + 1) * IN_SUB] for s in IN_PERM], axis=1).astype(BF16)
        wab = jnp.concatenate([lru_wa[li], lru_wx[li]], axis=-1).astype(BF16)
        bab = jnp.concatenate([lru_ba[li], lru_bx[li]], axis=0)
        sink_b = jnp.broadcast_to(sink[li][:, None], (N_HEADS, LANES))
        g1 = g_mix[li].reshape(1, D)
        g2 = g_ffn[li].reshape(1, D)
        outs = [w_a_out[li].astype(BF16), w_b_out[li].astype(BF16), w_c_out[li].astype(BF16),
                w_d_out[li].astype(BF16), w_out[li].astype(BF16)]
        bout_consts = (conv_a[li], conv_d[li], conv_d_b[li].reshape(1, D), ln_d_g[li].reshape(1, D),
                       ln_d_b[li].reshape(1, D))
        wi = w_ffn_in[li].astype(BF16)
        wo = w_ffn_out[li].astype(BF16)
        ml = mods[li]
        lru_consts = (conv_c[li], conv_c_b[li], wab, bab, lru_lam[li])

        pc = _inproj_call(xc.reshape(1, b * lc, D), ml, ctx_rowf, g1, None, w_p, b * lc).reshape(b, lc, N_IN)
        hcf, hcb, hcl = _lru_call(pc, jnp.zeros((b, 2, SUBLANES, LANES), F32), *lru_consts, lc)
        p = _inproj_call(x, ml, lat_row, g1, tab, w_p, _seq_tile(l, 1024))
        hf, hb, _ = _lru_call(p, hcl, *lru_consts, _seq_tile(l, 512))
        yb = _attn_call(p, pc, sink_b, _seq_tile(l, 512))
        x = _bout_call(p, hf, hb, yb, x, ml, lat_row, *bout_consts, *outs, _seq_tile(l, 256))
        x = _ffn_call(x, ml, lat_row, g2, wi, wo, g_fin, li == DEPTH - 1, _seq_tile(l, 512))
        if need_ctx:
            ybc = _cattn_call(pc, sink_b)
            xc = _bout_call(pc, hcf, hcb, ybc, xc, ml, ctx_rowf, *bout_consts, *outs, lc)
            xc = _ffn_call(xc.reshape(1, b * lc, D), ml, ctx_rowf, g2, wi, wo, g_fin, False,
                           _seq_tile(b * lc, 512)).reshape(b, lc, D)
    return x
```

```python
import functools

import jax
import jax.numpy as jnp
from jax import lax
from jax.experimental import pallas as pl
from jax.experimental.pallas import tpu as pltpu

F32 = jnp.float32
BF16 = jnp.bfloat16

D = 1024
DEPTH = 4
GRID_W = 64
EPS = 1e-6
NEG_INF = -1e30
F32_MIN_NORMAL = 1.17549435e-38
K_A = 3
N_HEADS = 8
N_KV = 2
HEAD_DIM = 128
GROUP = N_HEADS // N_KV
WINDOW = 128
ROPE_THETA = 10000.0
K_C = 4
LRU_BLOCKS = 8
LRU_BS = D // LRU_BLOCKS
LRU_C = 8.0
K_D = 31
D_FF = 2816
N_IN = 12800

C_MA, C_AB, C_MD, C_RG, C_RU, C_GL, C_Q = 0, 2, 3, 5, 6, 7, 11
C_K, C_V = 48, 49
HALO = 16
SUBLANES = 8
LANES = 128
N_CHUNK = D // LANES
SLAB_PAD = 8
VMEM_LIMIT = 56 * 1024 * 1024


def _cparams(sem):
    return pltpu.CompilerParams(dimension_semantics=sem, vmem_limit_bytes=VMEM_LIMIT)


def _resident(shape):
    nd = len(shape)
    return pl.BlockSpec(shape, lambda *_: (0,) * nd, pipeline_mode=pl.Buffered(1))


def _modnorm(x, g, shift, scale):
    y = x * lax.rsqrt(jnp.mean(x * x, axis=-1, keepdims=True) + EPS)
    return (y * g) * (1.0 + scale) + shift


def _sigmoid(x):
    return 0.5 * jnp.tanh(0.5 * x) + 0.5


def _mod_kernel(c_ref, w_ref, b_ref, o_ref):
    s = c_ref[...]
    s = s * _sigmoid(s)
    w = w_ref[...]
    s_hi = s.astype(BF16)
    s_lo = (s - s_hi.astype(F32)).astype(BF16)
    w_hi = w.astype(BF16)
    w_lo = (w - w_hi.astype(F32)).astype(BF16)
    acc = jnp.dot(s_hi, w_hi, preferred_element_type=F32)
    acc += jnp.dot(s_lo, w_hi, preferred_element_type=F32)
    acc += jnp.dot(s_hi, w_lo, preferred_element_type=F32)
    o_ref[...] = acc + b_ref[...]


def _mod_call(cc, w_mod, b_mod):
    tn = 1536
    return pl.pallas_call(
        _mod_kernel,
        grid=(DEPTH, 6 * D // tn),
        in_specs=[
            pl.BlockSpec((8, D), lambda l, j: (0, 0)),
            pl.BlockSpec((None, D, tn), lambda l, j: (l, 0, j)),
            pl.BlockSpec((None, 1, tn), lambda l, j: (l, 0, j)),
        ],
        out_specs=pl.BlockSpec((None, 8, tn), lambda l, j: (l, 0, j)),
        out_shape=jax.ShapeDtypeStruct((DEPTH, 8, 6 * D), F32),
        compiler_params=_cparams(("arbitrary", "arbitrary")),
    )(cc, w_mod, b_mod.reshape(DEPTH, 1, 6 * D))


IN_SUB = 512
IN_GROUP = 5
IN_TN = IN_SUB * IN_GROUP
Q_SCALE = HEAD_DIM ** -0.5
IN_PERM = (2, 4, 3, 5, 0, 1, 13, 15, 14, 16, 9, 10, 11, 12, 17, 18, 19, 20, 21, 22, 23, 24, 6, 7, 8)


def _rope(t, tab_ref):
    cos = tab_ref[:, 0:LANES]
    sin_lo = tab_ref[:, LANES:2 * LANES]
    sin_hi = tab_ref[:, 2 * LANES:3 * LANES]
    return t * cos + pltpu.roll(t, 96, 1) * sin_lo + pltpu.roll(t, 32, 1) * sin_hi


def _inproj_kernel(*refs, rope):
    if rope:
        x_ref, mod_ref, g_ref, tab_ref, w_ref, o_ref, h_ref = refs
    else:
        x_ref, mod_ref, g_ref, w_ref, o_ref, h_ref = refs
        tab_ref = None
    j = pl.program_id(2)

    @pl.when(j == 0)
    def _():
        h_ref[...] = _modnorm(x_ref[...], g_ref[...], mod_ref[0:1, :], mod_ref[1:2, :]).astype(BF16)

    def mm(s):
        return jnp.dot(h_ref[...], w_ref[:, s * IN_SUB:(s + 1) * IN_SUB], preferred_element_type=F32)

    def put(pos, val):
        o_ref[:, pos * IN_SUB:(pos + 1) * IN_SUB] = val.astype(BF16)

    def zero(pos):
        o_ref[:, pos * IN_SUB:(pos + 1) * IN_SUB] = jnp.zeros((o_ref.shape[0], IN_SUB), BF16)

    def rotated(acc, n_rot, scale):
        for c in range(IN_SUB // LANES):
            t = acc[:, c * LANES:(c + 1) * LANES]
            if rope and c < n_rot:
                t = _rope(t, tab_ref)
            if scale != 1.0:
                t = t * scale
            yield c, t

    @pl.when(j == 0)
    def _():
        put(0, mm(0) * mm(1))
        put(1, mm(2) * mm(3))
        zero(2)
        zero(3)
        put(4, mm(4))

    @pl.when(j == 1)
    def _():
        put(0, mm(0))
        put(1, mm(1) * _sigmoid(mm(2)))
        put(2, mm(3) * _sigmoid(mm(4)))
        zero(3)
        zero(4)

    @pl.when(j == 2)
    def _():
        put(0, jax.nn.gelu(mm(0)))
        put(1, jax.nn.gelu(mm(1)))
        put(2, mm(2))
        put(3, mm(3))
        put(4, _sigmoid(mm(4)))

    @pl.when(j == 3)
    def _():
        for s in range(IN_GROUP):
            put(s, _sigmoid(mm(s)))

    @pl.when(j == 4)
    def _():
        put(0, _sigmoid(mm(0)))
        put(1, _sigmoid(mm(1)))
        n_chunk = IN_SUB // LANES
        for s, n_rot, scale in ((2, n_chunk, Q_SCALE), (3, n_chunk, Q_SCALE), (4, n_chunk // 2, 1.0)):
            for c, t in rotated(mm(s), n_rot, scale):
                o_ref[:, s * IN_SUB + c * LANES:s * IN_SUB + (c + 1) * LANES] = t.astype(BF16)


def _inproj_call(x, mods_l, mod_row, g, tab, w, tm):
    bx, lx, _ = x.shape
    rope = tab is not None
    in_specs = [
        pl.BlockSpec((None, tm, D), lambda b, i, j: (b, i, 0)),
        pl.BlockSpec((None, 6, D), lambda b, i, j: (mod_row(b), 0, 0)),
        pl.BlockSpec((1, D), lambda b, i, j: (0, 0)),
    ]
    args = [x, mods_l, g]
    if rope:
        in_specs.append(pl.BlockSpec((tm, 3 * LANES), lambda b, i, j: (i, 0)))
        args.append(tab)
    in_specs.append(pl.BlockSpec((D, IN_TN), lambda b, i, j: (0, j)))
    args.append(w)
    return pl.pallas_call(
        functools.partial(_inproj_kernel, rope=rope),
        grid=(bx, lx // tm, N_IN // IN_TN),
        in_specs=in_specs,
        out_specs=pl.BlockSpec((None, tm, IN_TN), lambda b, i, j: (b, i, j)),
        out_shape=jax.ShapeDtypeStruct((bx, lx, N_IN), BF16),
        scratch_shapes=[pltpu.VMEM((tm, D), BF16)],
        compiler_params=_cparams(("arbitrary", "arbitrary", "arbitrary")),
    )(*args)


def _pcol(t, c, width=D):
    return pl.BlockSpec((None, t, width), lambda b, i: (b, i, c))


def _halo_specs(t, n_t, c, tile_of=lambda i: i):
    per = t // HALO
    last = n_t * per - 1
    return [
        pl.BlockSpec((None, HALO, D), lambda b, i: (b, jnp.maximum(tile_of(i) * per - 1, 0), c)),
        pl.BlockSpec((None, t, D), lambda b, i: (b, tile_of(i), c)),
        pl.BlockSpec((None, HALO, D), lambda b, i: (b, jnp.minimum((tile_of(i) + 1) * per, last), c)),
    ]


def _edge_masks(tile, n_t):
    has_prev = jnp.where(tile > 0, 1.0, 0.0).astype(F32)
    has_next = jnp.where(tile < n_t - 1, 1.0, 0.0).astype(F32)
    return has_prev, has_next


def _tile_rows(j, pitch):
    return (pl.ds(j, N_CHUNK, stride=pitch), slice(None))


def _fill_ext_slab(slab, t, prev, cur, nxt):
    pitch = t + 2 * HALO + SLAB_PAD
    for c in range(N_CHUNK):
        cs = slice(c * LANES, (c + 1) * LANES)
        slab[c * pitch:c * pitch + HALO, :] = prev(cs)
        slab[c * pitch + HALO:c * pitch + HALO + t, :] = cur(cs)
        slab[c * pitch + HALO + t:c * pitch + 2 * HALO + t, :] = nxt(cs)


def _loop(n, body, inline, unroll=1):
    if inline:
        for i in range(n):
            body(i, 0)
    else:
        lax.fori_loop(0, n, body, 0, unroll=unroll)


def _fold(slab, pitch, dst, n_rows, inline=False):
    def body(r, carry):
        dst[r] = slab[_tile_rows(r, pitch)]
        return carry

    _loop(n_rows, body, inline, unroll=8)


def _conv_folded(ext_f, w_ref, k_taps, t, emit, rows_per_iter=4, inline=False):
    off = HALO - k_taps // 2

    def body(it, carry):
        t0 = it * rows_per_iter
        loaded = {}

        def x(r):
            if r not in loaded:
                loaded[r] = ext_f[t0 + (off + r)]
            return loaded[r]

        n_part = min(4, k_taps)
        parts = [[None] * n_part for _ in range(rows_per_iter)]
        for k in range(k_taps):
            w = w_ref[k]
            for r in range(rows_per_iter):
                term = w * x(k + r)
                prev = parts[r][k % n_part]
                parts[r][k % n_part] = term if prev is None else prev + term
        for r in range(rows_per_iter):
            acc = parts[r]
            while len(acc) > 1:
                acc = [acc[i] + acc[i + 1] for i in range(0, len(acc) - 1, 2)] + ([acc[-1]] if len(acc) % 2 else [])
            emit(t0 + r, acc[0])
        return carry

    _loop(t // rows_per_iter, body, inline)


SCAN_GROUP = 4


def _lru_kernel(pfp, pfc, pfn, pbp, pbc, pbn, cw_ref, cb_ref, wab_ref, bab_ref, lam_ref, h0_ref,
                hf_ref, hb_ref, hl_ref, slab_e0, ext_f0, slab_u0, slab_e1, ext_f1, slab_u1, af, bf, ab, bb, carry,
                *, t, n_t):
    i = pl.program_id(1)
    hf_slab, hb_slab = slab_u0, slab_u1
    pitch_e = t + 2 * HALO + SLAB_PAD
    pitch = t + SLAB_PAD

    @pl.when(i == 0)
    def _():
        carry[...] = h0_ref[...]

    for d, (prv, cur, nxt, a_slab, b_slab, slab_e, ext_f, slab_u) in enumerate((
            (pfp, pfc, pfn, af, bf, slab_e0, ext_f0, slab_u0), (pbp, pbc, pbn, ab, bb, slab_e1, ext_f1, slab_u1))):
        tile = i if d == 0 else n_t - 1 - i
        has_prev, has_next = _edge_masks(tile, n_t)
        _fill_ext_slab(slab_e, t,
                       lambda cs: prv[:, cs].astype(F32) * has_prev,
                       lambda cs: cur[:, cs].astype(F32),
                       lambda cs: nxt[:, cs].astype(F32) * has_next)
        _fold(slab_e, pitch_e, ext_f, t + 2 * HALO, inline=True)

        bias = cb_ref[...]

        def emit(row, tile_val):
            slab_u[_tile_rows(row, pitch)] = tile_val + bias

        _conv_folded(ext_f, cw_ref, K_C, t, emit, inline=True)

        neg_lam = -lam_ref[d:d + 1, :]
        softplus = jnp.maximum(neg_lam, 0.0) + jnp.log1p(jnp.exp(-jnp.abs(neg_lam)))
        half_c8 = (-0.5 * LRU_C) * softplus
        for n in range(LRU_BLOCKS):
            cs = slice(n * LRU_BS, (n + 1) * LRU_BS)
            rs = slice(n * pitch, n * pitch + t)
            u = slab_u[rs, :]
            g = jnp.dot(u.astype(BF16), wab_ref[d, n], preferred_element_type=F32)
            t_r = jnp.tanh(g[:, :LRU_BS] + bab_ref[d:d + 1, cs])
            t_i = jnp.tanh(g[:, LRU_BS:] + bab_ref[2 + d:3 + d, cs])
            log_a = half_c8[:, cs] * t_r + half_c8[:, cs]
            a = jnp.exp(log_a)
            y = jnp.tanh(log_a) * (-1.0 - a * a)
            gain = y * lax.rsqrt(jnp.maximum(y, F32_MIN_NORMAL))
            a_slab[rs, :] = a
            b_slab[rs, :] = gain * ((0.5 * t_i + 0.5) * u)

    def chain(a_slab, b_slab, h_slab, h, positions):
        pa = pb = out = None
        for pos in positions:
            rows = _tile_rows(pos, pitch)
            a = a_slab[rows]
            b = b_slab[rows]
            if pa is None:
                pa, pb = a, b
            else:
                pa, pb = a * pa, a * pb + b
            out = pa * h + pb
            h_slab[rows] = out
        return out

    def step(it, c):
        h_f, h_b = c
        j0 = it * SCAN_GROUP
        h_f = chain(af, bf, hf_slab, h_f, [j0 + g for g in range(SCAN_GROUP)])
        h_b = chain(ab, bb, hb_slab, h_b, [t - 1 - j0 - g for g in range(SCAN_GROUP)])
        return h_f, h_b

    h_f, h_b = lax.fori_loop(0, t // SCAN_GROUP, step, (carry[0], carry[1]), unroll=2)
    carry[0] = h_f
    carry[1] = h_b
    hl_ref[...] = carry[...]
    for n in range(N_CHUNK):
        cs = slice(n * LANES, (n + 1) * LANES)
        rs = slice(n * pitch, n * pitch + t)
        hf_ref[:, cs] = hf_slab[rs, :].astype(BF16)
        hb_ref[:, cs] = hb_slab[rs, :].astype(BF16)


def _lru_call(p, h0, conv_c, conv_c_b, wab, bab, lam, t):
    b, l, _ = p.shape
    n_t = l // t
    rev = lambda i: n_t - 1 - i
    small = lambda shape: pl.BlockSpec(shape, lambda b, i: (0,) * len(shape))
    state = lambda: pl.BlockSpec((None, 2, SUBLANES, LANES), lambda b, i: (b, 0, 0, 0))
    slab = lambda rows: pltpu.VMEM((N_CHUNK * (rows + SLAB_PAD), LANES), F32)
    return pl.pallas_call(
        functools.partial(_lru_kernel, t=t, n_t=n_t),
        grid=(b, n_t),
        in_specs=(_halo_specs(t, n_t, C_RU) + _halo_specs(t, n_t, C_RU, rev) + [
            small((K_C, SUBLANES, LANES)), small((SUBLANES, LANES)),
            small((2, LRU_BLOCKS, LRU_BS, 2 * LRU_BS)), small((4, D)), small((2, D)), state()]),
        out_specs=[
            pl.BlockSpec((None, t, D), lambda b, i: (b, i, 0)),
            pl.BlockSpec((None, t, D), lambda b, i: (b, rev(i), 0)),
            state(),
        ],
        out_shape=[
            jax.ShapeDtypeStruct((b, l, D), BF16),
            jax.ShapeDtypeStruct((b, l, D), BF16),
            jax.ShapeDtypeStruct((b, 2, SUBLANES, LANES), F32),
        ],
        scratch_shapes=[
            slab(t + 2 * HALO), pltpu.VMEM((t + 2 * HALO, SUBLANES, LANES), F32), slab(t),
            slab(t + 2 * HALO), pltpu.VMEM((t + 2 * HALO, SUBLANES, LANES), F32), slab(t),
            slab(t), slab(t), slab(t), slab(t), pltpu.VMEM((2, SUBLANES, LANES), F32),
        ],
        compiler_params=_cparams(("arbitrary", "arbitrary")),
    )(p, p, p, p, p, p, conv_c.reshape(K_C, SUBLANES, LANES), conv_c_b.reshape(SUBLANES, LANES), wab, bab, lam, h0)


def _sink_column(sink_ref, kh, rows_per_head):
    rid = lax.broadcasted_iota(jnp.int32, (GROUP * rows_per_head, 1), 0)
    col = sink_ref[kh * GROUP + GROUP - 1:kh * GROUP + GROUP, 0:1]
    for g in range(GROUP - 2, -1, -1):
        col = jnp.where(rid < (g + 1) * rows_per_head, sink_ref[kh * GROUP + g:kh * GROUP + g + 1, 0:1], col)
    return col


def _qk(q, k):
    return lax.dot_general(q, k, (((1,), (1,)), ((), ())), preferred_element_type=F32)


def _stack_heads(q_ref, rs, kh):
    return jnp.concatenate(
        [q_ref[rs, (kh * GROUP + g) * HEAD_DIM:(kh * GROUP + g + 1) * HEAD_DIM] for g in range(GROUP)], axis=0)


def _attn_kernel(q_ref, k_ref, v_ref, kc_ref, vc_ref, sink_ref, o_ref, kpad, vpad, vcx, *, tq, l):
    i = pl.program_id(1)
    blk = WINDOW
    nqb = tq // blk
    nb = l // blk
    rows = GROUP * blk
    vw = 2 * HEAD_DIM

    @pl.when(i == 0)
    def _():
        kpad[0:blk, :] = jnp.zeros((blk, N_KV * HEAD_DIM), BF16)
        kpad[blk:blk + l, :] = k_ref[...]
        kpad[blk + l:2 * blk + l, :] = jnp.zeros((blk, N_KV * HEAD_DIM), BF16)
        vpad[0:blk, :] = jnp.zeros((blk, N_KV * vw), BF16)
        vpad[blk + l:2 * blk + l, :] = jnp.zeros((blk, N_KV * vw), BF16)
        for kh in range(N_KV):
            vpad[blk:blk + l, kh * vw:kh * vw + HEAD_DIM] = v_ref[:, kh * HEAD_DIM:(kh + 1) * HEAD_DIM]
            vpad[blk:blk + l, kh * vw + HEAD_DIM:(kh + 1) * vw] = jnp.ones((l, HEAD_DIM), BF16)
            vcx[:, kh * vw:kh * vw + HEAD_DIM] = vc_ref[:, kh * HEAD_DIM:(kh + 1) * HEAD_DIM]
            vcx[:, kh * vw + HEAD_DIM:(kh + 1) * vw] = jnp.ones((vc_ref.shape[0], HEAD_DIM), BF16)

    qrow = lax.broadcasted_iota(jnp.int32, (rows, blk), 0) & (blk - 1)
    kcol = lax.broadcasted_iota(jnp.int32, (rows, blk), 1)
    bias_prev = jnp.where(kcol >= qrow, 0.0, NEG_INF).astype(F32)
    bias_next = jnp.where(kcol <= qrow, 0.0, NEG_INF).astype(F32)

    units = [(qb, kh) for qb in range(nqb) for kh in range(N_KV)]

    def scores(qb, kh):
        n = i * nqb + qb
        start = pl.multiple_of(n * blk, blk)
        ks = slice(kh * HEAD_DIM, (kh + 1) * HEAD_DIM)
        q4 = _stack_heads(q_ref, slice(qb * blk, (qb + 1) * blk), kh)
        s_loc = _qk(q4, kpad[pl.ds(start, 3 * blk), ks])
        s_ctx = _qk(q4, kc_ref[:, ks])
        no_prev = jnp.where(n > 0, 0.0, NEG_INF).astype(F32)
        no_next = jnp.where(n < nb - 1, 0.0, NEG_INF).astype(F32)
        return (s_loc[:, 0:blk] + (bias_prev + no_prev), s_loc[:, blk:2 * blk],
                s_loc[:, 2 * blk:3 * blk] + (bias_next + no_next), s_ctx)

    def finish(qb, kh, s_prev, s_own, s_next, s_ctx):
        n = i * nqb + qb
        start = pl.multiple_of(n * blk, blk)
        s_sink = _sink_column(sink_ref, kh, blk)
        m_el = jnp.maximum(jnp.maximum(s_prev, s_own), s_next)
        for c in range(s_ctx.shape[1] // blk):
            m_el = jnp.maximum(m_el, s_ctx[:, c * blk:(c + 1) * blk])
        m = jnp.maximum(jnp.max(m_el, axis=-1, keepdims=True), s_sink)
        p_loc = jnp.concatenate([jnp.exp(s_prev - m).astype(BF16), jnp.exp(s_own - m).astype(BF16),
                                 jnp.exp(s_next - m).astype(BF16)], axis=1)
        p_ctx = jnp.exp(s_ctx - m).astype(BF16)
        vs = slice(kh * vw, (kh + 1) * vw)
        o = jnp.dot(p_loc, vpad[pl.ds(start, 3 * blk), vs], preferred_element_type=F32)
        o += jnp.dot(p_ctx, vcx[:, vs], preferred_element_type=F32)
        den = o[:, HEAD_DIM:vw] + jnp.exp(s_sink - m)
        o = o[:, 0:HEAD_DIM] / den
        for g in range(GROUP):
            hs = slice((kh * GROUP + g) * HEAD_DIM, (kh * GROUP + g + 1) * HEAD_DIM)
            o_ref[qb * blk:(qb + 1) * blk, hs] = o[g * blk:(g + 1) * blk, :].astype(BF16)

    pending = scores(*units[0])
    for n_u, unit in enumerate(units):
        nxt = scores(*units[n_u + 1]) if n_u + 1 < len(units) else None
        finish(*unit, *pending)
        pending = nxt


def _attn_call(p, pc, sink_b, tq):
    b, l, _ = p.shape
    lc = pc.shape[1]
    kvw = N_KV * HEAD_DIM
    return pl.pallas_call(
        functools.partial(_attn_kernel, tq=tq, l=l),
        grid=(b, l // tq),
        in_specs=[
            _pcol(tq, C_Q),
            pl.BlockSpec((None, l, kvw), lambda b, i: (b, 0, C_K)),
            pl.BlockSpec((None, l, kvw), lambda b, i: (b, 0, C_V)),
            pl.BlockSpec((None, lc, kvw), lambda b, i: (b, 0, C_K)),
            pl.BlockSpec((None, lc, kvw), lambda b, i: (b, 0, C_V)),
            pl.BlockSpec((N_HEADS, LANES), lambda b, i: (0, 0)),
        ],
        out_specs=pl.BlockSpec((None, tq, D), lambda b, i: (b, i, 0)),
        out_shape=jax.ShapeDtypeStruct((b, l, D), BF16),
        scratch_shapes=[pltpu.VMEM((l + 2 * WINDOW, kvw), BF16), pltpu.VMEM((l + 2 * WINDOW, 2 * kvw), BF16),
                        pltpu.VMEM((lc, 2 * kvw), BF16)],
        compiler_params=_cparams(("arbitrary", "arbitrary")),
    )(p, p, p, pc, pc, sink_b)


def _cattn_kernel(q_ref, kc_ref, vc_ref, sink_ref, o_ref, *, lc):
    for kh in range(N_KV):
        ks = slice(kh * HEAD_DIM, (kh + 1) * HEAD_DIM)
        q4 = _stack_heads(q_ref, slice(0, lc), kh)
        s = _qk(q4, kc_ref[:, ks])
        s_sink = _sink_column(sink_ref, kh, lc)
        m = jnp.maximum(jnp.max(s, axis=-1, keepdims=True), s_sink)
        p = jnp.exp(s - m)
        den = jnp.sum(p, axis=-1, keepdims=True) + jnp.exp(s_sink - m)
        o = jnp.dot(p.astype(BF16), vc_ref[:, ks], preferred_element_type=F32) / den
        for g in range(GROUP):
            hs = slice((kh * GROUP + g) * HEAD_DIM, (kh * GROUP + g + 1) * HEAD_DIM)
            o_ref[:, hs] = o[g * lc:(g + 1) * lc, :].astype(BF16)


def _cattn_call(pc, sink_b):
    b, lc, _ = pc.shape
    kvw = N_KV * HEAD_DIM
    return pl.pallas_call(
        functools.partial(_cattn_kernel, lc=lc),
        grid=(b,),
        in_specs=[
            pl.BlockSpec((None, lc, D), lambda b: (b, 0, C_Q)),
            pl.BlockSpec((None, lc, kvw), lambda b: (b, 0, C_K)),
            pl.BlockSpec((None, lc, kvw), lambda b: (b, 0, C_V)),
            pl.BlockSpec((N_HEADS, LANES), lambda b: (0, 0)),
        ],
        out_specs=pl.BlockSpec((None, lc, D), lambda b: (b, 0, 0)),
        out_shape=jax.ShapeDtypeStruct((b, lc, D), BF16),
        compiler_params=_cparams(("arbitrary",)),
    )(pc, pc, pc, sink_b)


def _bout_kernel(ab_ref, map_, mac, man, rg_ref, mdp, mdc, mdn, ga_ref, gb_ref, gc_ref, gd_ref,
                 hf_ref, hb_ref, yb_ref, x_ref, mod_ref,
                 ca_ref, cd_ref, cdb_ref, lng_ref, lnb_ref, wa_ref, wb_ref, wc_ref, wd_ref, wo_ref,
                 o_ref, slab_e, ext_f, slab_z, z_scr, *, t, n_t):
    i = pl.program_id(1)
    has_prev, has_next = _edge_masks(i, n_t)
    pitch_e = t + 2 * HALO + SLAB_PAD
    pitch = t + SLAB_PAD

    def f32(ref):
        return ref[...].astype(F32)

    def conv(k_taps, w_ref, prv, cur, nxt):
        _fill_ext_slab(slab_e, t,
                       lambda cs: prv[:, cs].astype(F32) * has_prev,
                       lambda cs: cur[:, cs].astype(F32),
                       lambda cs: nxt[:, cs].astype(F32) * has_next)
        _fold(slab_e, pitch_e, ext_f, t + 2 * HALO, inline=True)

        def emit(row, tile_val):
            slab_z[_tile_rows(row, pitch)] = tile_val

        _conv_folded(ext_f, w_ref, k_taps, t, emit, inline=True)
        for c in range(N_CHUNK):
            z_scr[:, c * LANES:(c + 1) * LANES] = slab_z[c * pitch:c * pitch + t, :]

    def gated(gate_ref, pre, w_ref):
        return f32(gate_ref) * jnp.dot(pre.astype(BF16), w_ref[...], preferred_element_type=F32)

    conv(K_A, ca_ref, map_, mac, man)
    merged = gated(ga_ref, f32(ab_ref) * z_scr[...], wa_ref)
    merged += gated(gb_ref, yb_ref[...], wb_ref)
    merged += gated(gc_ref, f32(rg_ref) * (f32(hf_ref) + f32(hb_ref)), wc_ref)
    conv(K_D, cd_ref, mdp, mdc, mdn)
    z = z_scr[...] + cdb_ref[...]
    mu = jnp.mean(z, axis=-1, keepdims=True)
    zc = z - mu
    var = jnp.mean(zc * zc, axis=-1, keepdims=True)
    y = zc * lax.rsqrt(var + EPS) * lng_ref[...] + lnb_ref[...]
    merged += gated(gd_ref, y * _sigmoid(y), wd_ref)

    o = jnp.dot(merged.astype(BF16), wo_ref[...], preferred_element_type=F32)
    o_ref[...] = x_ref[...] + mod_ref[2:3, :] * o


def _bout_call(p, hf, hb, yb, x, mods_l, mod_row, conv_a, conv_d, conv_d_b, ln_g, ln_b, wa, wb, wc, wd, wo, t):
    b, l, _ = p.shape
    n_t = l // t
    row = lambda: pl.BlockSpec((None, t, D), lambda b, i: (b, i, 0))
    p_specs = ([_pcol(t, C_AB)] + _halo_specs(t, n_t, C_MA) + [_pcol(t, C_RG)] + _halo_specs(t, n_t, C_MD)
               + [_pcol(t, C_GL + k) for k in range(4)])
    in_specs = (p_specs
                + [row(), row(), row(), row(),
                   pl.BlockSpec((None, 6, D), lambda b, i: (mod_row(b), 0, 0)),
                   _resident((K_A, SUBLANES, LANES)), _resident((K_D, SUBLANES, LANES)),
                   _resident((1, D)), _resident((1, D)), _resident((1, D))]
                + [_resident((D, D)) for _ in range(5)])
    slab = lambda rows: pltpu.VMEM((N_CHUNK * (rows + SLAB_PAD), LANES), F32)
    return pl.pallas_call(
        functools.partial(_bout_kernel, t=t, n_t=n_t),
        grid=(b, n_t),
        in_specs=in_specs,
        out_specs=pl.BlockSpec((None, t, D), lambda b, i: (b, i, 0)),
        out_shape=jax.ShapeDtypeStruct((b, l, D), F32),
        scratch_shapes=[slab(t + 2 * HALO), pltpu.VMEM((t + 2 * HALO, SUBLANES, LANES), F32), slab(t),
                        pltpu.VMEM((t, D), F32)],
        compiler_params=_cparams(("arbitrary", "arbitrary")),
    )(*([p] * len(p_specs)), hf, hb, yb, x, mods_l, conv_a.reshape(K_A, SUBLANES, LANES),
      conv_d.reshape(K_D, SUBLANES, LANES), conv_d_b, ln_g, ln_b, wa, wb, wc, wd, wo)


FF_CHUNK = 256


def _ffn_kernel(x_ref, mod_ref, g_ref, wi_ref, wo_ref, gf_ref, o_ref, acc, *, final):
    x = x_ref[...]
    h = _modnorm(x, g_ref[...], mod_ref[3:4, :], mod_ref[4:5, :]).astype(BF16)
    for c in range(D_FF // FF_CHUNK):
        gate = jnp.dot(h, wi_ref[:, c * FF_CHUNK:(c + 1) * FF_CHUNK], preferred_element_type=F32)
        up = jnp.dot(h, wi_ref[:, D_FF + c * FF_CHUNK:D_FF + (c + 1) * FF_CHUNK], preferred_element_type=F32)
        act = (gate * _sigmoid(gate) * up).astype(BF16)
        part = jnp.dot(act, wo_ref[c * FF_CHUNK:(c + 1) * FF_CHUNK, :], preferred_element_type=F32)
        if c == 0:
            acc[...] = part
        else:
            acc[...] += part
    y = x + mod_ref[5:6, :] * acc[...]
    if final:
        y = y * lax.rsqrt(jnp.mean(y * y, axis=-1, keepdims=True) + EPS) * gf_ref[...]
    o_ref[...] = y


def _ffn_call(x, mods_l, mod_row, g, wi, wo, g_final, final, t):
    b, l, _ = x.shape
    return pl.pallas_call(
        functools.partial(_ffn_kernel, final=final),
        grid=(b, l // t),
        in_specs=[
            pl.BlockSpec((None, t, D), lambda b, i: (b, i, 0)),
            pl.BlockSpec((None, 6, D), lambda b, i: (mod_row(b), 0, 0)),
            _resident((1, D)), _resident((D, 2 * D_FF)), _resident((D_FF, D)), _resident((1, D)),
        ],
        out_specs=pl.BlockSpec((None, t, D), lambda b, i: (b, i, 0)),
        out_shape=jax.ShapeDtypeStruct((b, l, D), F32),
        scratch_shapes=[pltpu.VMEM((t, D), F32)],
        compiler_params=_cparams(("arbitrary", "arbitrary")),
    )(x, mods_l, g, wi, wo, g_final)


def _rope_table(n_tok):
    rows = n_tok // GRID_W
    half = HEAD_DIM // 2
    inv_freq = ROPE_THETA ** (-jnp.arange(0, half, 2, dtype=F32) / half)
    ang_r = jnp.arange(rows, dtype=F32)[:, None] * inv_freq[None, :]
    ang_c = jnp.arange(GRID_W, dtype=F32)[:, None] * inv_freq[None, :]

    def table(f):
        r = jnp.broadcast_to(f(ang_r)[:, None, :], (rows, GRID_W, half // 2))
        c = jnp.broadcast_to(f(ang_c)[None, :, :], (rows, GRID_W, half // 2))
        return jnp.concatenate([r, r, c, c], axis=-1).reshape(n_tok, HEAD_DIM)

    cos, sin = table(jnp.cos), table(jnp.sin)
    first = (jnp.arange(HEAD_DIM) % half) < half // 2
    return jnp.concatenate([cos, jnp.where(first, -sin, 0.0), jnp.where(first, 0.0, sin)], axis=-1)


def _seq_tile(l, want):
    return min(l, want)


def kernel(x, c, ctx, c_ctx, w_mod, b_mod, g_mix, w_in, conv_a, sink, conv_c, conv_c_b, lru_wa, lru_ba, lru_wx,
           lru_bx, lru_lam, conv_d, conv_d_b, ln_d_g, ln_d_b, w_a_out, w_b_out, w_c_out, w_d_out, w_out, g_ffn,
           w_ffn_in, w_ffn_out, g_final):
    b, l, _ = x.shape
    lc = ctx.shape[1]
    ctx_row = b
    cc = jnp.zeros((8, D), F32).at[:b].set(c).at[ctx_row].set(c_ctx)
    mods = _mod_call(cc, w_mod, b_mod).reshape(DEPTH, 8, 6, D)
    tab = _rope_table(l)
    lat_row = lambda bi: bi
    ctx_rowf = lambda bi: ctx_row
    g_fin = g_final.reshape(1, D)

    xc = ctx
    for li in range(DEPTH):
        need_ctx = li < DEPTH - 1
        w_p = jnp.concatenate([w_in[li][:, s * IN_SUB:(s + 1) * IN_SUB] for s in IN_PERM], axis=1).astype(BF16)
        wab = (0.5 * jnp.concatenate([lru_wa[li], lru_wx[li]], axis=-1)).astype(BF16)
        bab = 0.5 * jnp.concatenate([lru_ba[li], lru_bx[li]], axis=0)
        sink_b = jnp.broadcast_to(sink[li][:, None], (N_HEADS, LANES))
        g1 = g_mix[li].reshape(1, D)
        g2 = g_ffn[li].reshape(1, D)
        outs = [w_a_out[li].astype(BF16), w_b_out[li].astype(BF16), w_c_out[li].astype(BF16),
                w_d_out[li].astype(BF16), w_out[li].astype(BF16)]
        bout_consts = (conv_a[li], conv_d[li], conv_d_b[li].reshape(1, D), ln_d_g[li].reshape(1, D),
                       ln_d_b[li].reshape(1, D))
        wi = w_ffn_in[li].astype(BF16)
        wo = w_ffn_out[li].astype(BF16)
        ml = mods[li]
        lru_consts = (conv_c[li], conv_c_b[li], wab, bab, lru_lam[li])

        pc = _inproj_call(xc.reshape(1, b * lc, D), ml, ctx_rowf, g1, None, w_p, b * lc).reshape(b, lc, N_IN)
        hcf, hcb, hcl = _lru_call(pc, jnp.zeros((b, 2, SUBLANES, LANES), F32), *lru_consts, lc)
        p = _inproj_call(x, ml, lat_row, g1, tab, w_p, _seq_tile(l, 1024))
        hf, hb, _ = _lru_call(p, hcl, *lru_consts, _seq_tile(l, 512))
        yb = _attn_call(p, pc, sink_b, _seq_tile(l, 512))
        x = _bout_call(p, hf, hb, yb, x, ml, lat_row, *bout_consts, *outs, _seq_tile(l, 256))
        x = _ffn_call(x, ml, lat_row, g2, wi, wo, g_fin, li == DEPTH - 1, _seq_tile(l, 512))
        if need_ctx:
            ybc = _cattn_call(pc, sink_b)
            xc = _bout_call(pc, hcf, hcb, ybc, xc, ml, ctx_rowf, *bout_consts, *outs, lc)
            xc = _ffn_call(xc.reshape(1, b * lc, D), ml, ctx_rowf, g2, wi, wo, g_fin, False,
                           _seq_tile(b * lc, 512)).reshape(b, lc, D)
    return x
```

```python
import functools

import jax
import jax.numpy as jnp
from jax import lax
from jax.experimental import pallas as pl
from jax.experimental.pallas import tpu as pltpu

F32 = jnp.float32
BF16 = jnp.bfloat16

D = 1024
DEPTH = 4
GRID_W = 64
EPS = 1e-6
NEG_INF = -1e30
F32_MIN_NORMAL = 1.17549435e-38
K_A = 3
N_HEADS = 8
N_KV = 2
HEAD_DIM = 128
GROUP = N_HEADS // N_KV
WINDOW = 128
ROPE_THETA = 10000.0
K_C = 4
LRU_BLOCKS = 8
LRU_BS = D // LRU_BLOCKS
LRU_C = 8.0
K_D = 31
D_FF = 2816
N_IN = 12800

C_MA, C_AB, C_MD, C_RG, C_RU, C_GL, C_Q = 0, 2, 3, 5, 6, 7, 11
C_K, C_V = 48, 49
HALO = 16
SUBLANES = 8
LANES = 128
N_CHUNK = D // LANES
SLAB_PAD = 8
VMEM_LIMIT = 56 * 1024 * 1024


def _cparams(sem):
    return pltpu.CompilerParams(dimension_semantics=sem, vmem_limit_bytes=VMEM_LIMIT)


def _resident(shape):
    nd = len(shape)
    return pl.BlockSpec(shape, lambda *_: (0,) * nd, pipeline_mode=pl.Buffered(1))


def _resident_layer(shape, layer):
    nd = len(shape)
    return pl.BlockSpec((None,) + shape, lambda *_: (layer,) + (0,) * nd, pipeline_mode=pl.Buffered(1))


def _modnorm(x, g, shift, scale):
    y = x * lax.rsqrt(jnp.mean(x * x, axis=-1, keepdims=True) + EPS)
    return (y * g) * (1.0 + scale) + shift


def _sigmoid(x):
    return 0.5 * jnp.tanh(0.5 * x) + 0.5


def _mod_kernel(c_ref, w_ref, b_ref, o_ref):
    s = c_ref[...]
    s = s * _sigmoid(s)
    w = w_ref[...]
    s_hi = s.astype(BF16)
    s_lo = (s - s_hi.astype(F32)).astype(BF16)
    w_hi = w.astype(BF16)
    w_lo = (w - w_hi.astype(F32)).astype(BF16)
    acc = jnp.dot(s_hi, w_hi, preferred_element_type=F32)
    acc += jnp.dot(s_lo, w_hi, preferred_element_type=F32)
    acc += jnp.dot(s_hi, w_lo, preferred_element_type=F32)
    o_ref[...] = acc + b_ref[...]


def _mod_call(cc, w_mod, b_mod):
    tn = 1536
    return pl.pallas_call(
        _mod_kernel,
        grid=(DEPTH, 6 * D // tn),
        in_specs=[
            pl.BlockSpec((8, D), lambda l, j: (0, 0)),
            pl.BlockSpec((None, D, tn), lambda l, j: (l, 0, j)),
            pl.BlockSpec((None, 1, tn), lambda l, j: (l, 0, j)),
        ],
        out_specs=pl.BlockSpec((None, 8, tn), lambda l, j: (l, 0, j)),
        out_shape=jax.ShapeDtypeStruct((DEPTH, 8, 6 * D), F32),
        compiler_params=_cparams(("arbitrary", "arbitrary")),
    )(cc, w_mod, b_mod.reshape(DEPTH, 1, 6 * D))


IN_SUB = 512
IN_GROUP = 5
IN_TN = IN_SUB * IN_GROUP
Q_SCALE = HEAD_DIM ** -0.5
IN_PERM = (2, 4, 3, 5, 0, 1, 13, 15, 14, 16, 9, 10, 11, 12, 17, 18, 19, 20, 21, 22, 23, 24, 6, 7, 8)


def _rope(t, tab_ref):
    cos = tab_ref[:, 0:LANES]
    sin_lo = tab_ref[:, LANES:2 * LANES]
    sin_hi = tab_ref[:, 2 * LANES:3 * LANES]
    return t * cos + pltpu.roll(t, 96, 1) * sin_lo + pltpu.roll(t, 32, 1) * sin_hi


def _inproj_kernel(*refs, rope):
    x_ref, mod_ref, g_ref = refs[:3]
    tab_ref = refs[3] if rope else None
    w_refs = refs[-(IN_GROUP + 2):-2]
    o_ref, h_ref = refs[-2:]
    j = pl.program_id(2)

    @pl.when(j == 0)
    def _():
        h_ref[...] = _modnorm(x_ref[...], g_ref[...], mod_ref[0:1, :], mod_ref[1:2, :]).astype(BF16)

    def mm(s):
        return jnp.dot(h_ref[...], w_refs[s][...], preferred_element_type=F32)

    def put(pos, val):
        o_ref[:, pos * IN_SUB:(pos + 1) * IN_SUB] = val.astype(BF16)

    def zero(pos):
        o_ref[:, pos * IN_SUB:(pos + 1) * IN_SUB] = jnp.zeros((o_ref.shape[0], IN_SUB), BF16)

    def rotated(acc, n_rot, scale):
        for c in range(IN_SUB // LANES):
            t = acc[:, c * LANES:(c + 1) * LANES]
            if rope and c < n_rot:
                t = _rope(t, tab_ref)
            if scale != 1.0:
                t = t * scale
            yield c, t

    @pl.when(j == 0)
    def _():
        put(0, mm(0) * mm(1))
        put(1, mm(2) * mm(3))
        zero(2)
        zero(3)
        put(4, mm(4))

    @pl.when(j == 1)
    def _():
        put(0, mm(0))
        put(1, mm(1) * _sigmoid(mm(2)))
        put(2, mm(3) * _sigmoid(mm(4)))
        zero(3)
        zero(4)

    @pl.when(j == 2)
    def _():
        put(0, jax.nn.gelu(mm(0)))
        put(1, jax.nn.gelu(mm(1)))
        put(2, mm(2))
        put(3, mm(3))
        put(4, _sigmoid(mm(4)))

    @pl.when(j == 3)
    def _():
        for s in range(IN_GROUP):
            put(s, _sigmoid(mm(s)))

    @pl.when(j == 4)
    def _():
        put(0, _sigmoid(mm(0)))
        put(1, _sigmoid(mm(1)))
        n_chunk = IN_SUB // LANES
        for s, n_rot, scale in ((2, n_chunk, Q_SCALE), (3, n_chunk, Q_SCALE), (4, n_chunk // 2, 1.0)):
            for c, t in rotated(mm(s), n_rot, scale):
                o_ref[:, s * IN_SUB + c * LANES:s * IN_SUB + (c + 1) * LANES] = t.astype(BF16)


def _w_in_subtile(j, s):
    idx = IN_PERM[s]
    for grp in range(1, N_IN // IN_TN):
        idx = jnp.where(j == grp, IN_PERM[grp * IN_GROUP + s], idx)
    return idx


def _inproj_call(x, mods_l, mod_row, g, tab, w_all, layer, tm):
    bx, lx, _ = x.shape
    rope = tab is not None
    in_specs = [
        pl.BlockSpec((None, tm, D), lambda b, i, j: (b, i, 0)),
        pl.BlockSpec((None, 6, D), lambda b, i, j: (mod_row(b), 0, 0)),
        pl.BlockSpec((1, D), lambda b, i, j: (0, 0)),
    ]
    args = [x, mods_l, g]
    if rope:
        in_specs.append(pl.BlockSpec((tm, 3 * LANES), lambda b, i, j: (i, 0)))
        args.append(tab)
    for s in range(IN_GROUP):
        in_specs.append(pl.BlockSpec((None, D, IN_SUB), lambda b, i, j, s=s: (layer, 0, _w_in_subtile(j, s))))
        args.append(w_all)
    return pl.pallas_call(
        functools.partial(_inproj_kernel, rope=rope),
        grid=(bx, lx // tm, N_IN // IN_TN),
        in_specs=in_specs,
        out_specs=pl.BlockSpec((None, tm, IN_TN), lambda b, i, j: (b, i, j)),
        out_shape=jax.ShapeDtypeStruct((bx, lx, N_IN), BF16),
        scratch_shapes=[pltpu.VMEM((tm, D), BF16)],
        compiler_params=_cparams(("arbitrary", "arbitrary", "arbitrary")),
    )(*args)


def _pcol(t, c, width=D):
    return pl.BlockSpec((None, t, width), lambda b, i: (b, i, c))


def _halo_specs(t, n_t, c, tile_of=lambda i: i):
    per = t // HALO
    last = n_t * per - 1
    return [
        pl.BlockSpec((None, HALO, D), lambda b, i: (b, jnp.maximum(tile_of(i) * per - 1, 0), c)),
        pl.BlockSpec((None, t, D), lambda b, i: (b, tile_of(i), c)),
        pl.BlockSpec((None, HALO, D), lambda b, i: (b, jnp.minimum((tile_of(i) + 1) * per, last), c)),
    ]


def _edge_masks(tile, n_t):
    has_prev = jnp.where(tile > 0, 1.0, 0.0).astype(F32)
    has_next = jnp.where(tile < n_t - 1, 1.0, 0.0).astype(F32)
    return has_prev, has_next


def _tile_rows(j, pitch):
    return (pl.ds(j, N_CHUNK, stride=pitch), slice(None))


def _fill_ext_slab(slab, t, prev, cur, nxt):
    pitch = t + 2 * HALO + SLAB_PAD
    for c in range(N_CHUNK):
        cs = slice(c * LANES, (c + 1) * LANES)
        slab[c * pitch:c * pitch + HALO, :] = prev(cs)
        slab[c * pitch + HALO:c * pitch + HALO + t, :] = cur(cs)
        slab[c * pitch + HALO + t:c * pitch + 2 * HALO + t, :] = nxt(cs)


def _loop(n, body, inline, unroll=1):
    if inline:
        for i in range(n):
            body(i, 0)
    else:
        lax.fori_loop(0, n, body, 0, unroll=unroll)


def _fold(slab, pitch, dst, n_rows, inline=False):
    def body(r, carry):
        dst[r] = slab[_tile_rows(r, pitch)]
        return carry

    _loop(n_rows, body, inline, unroll=8)


def _conv_folded(ext_f, w_ref, k_taps, t, emit, rows_per_iter=4, inline=False):
    off = HALO - k_taps // 2

    def body(it, carry):
        t0 = it * rows_per_iter
        loaded = {}

        def x(r):
            if r not in loaded:
                loaded[r] = ext_f[t0 + (off + r)]
            return loaded[r]

        n_part = min(4, k_taps)
        parts = [[None] * n_part for _ in range(rows_per_iter)]
        for k in range(k_taps):
            w = w_ref[k]
            for r in range(rows_per_iter):
                term = w * x(k + r)
                prev = parts[r][k % n_part]
                parts[r][k % n_part] = term if prev is None else prev + term
        for r in range(rows_per_iter):
            acc = parts[r]
            while len(acc) > 1:
                acc = [acc[i] + acc[i + 1] for i in range(0, len(acc) - 1, 2)] + ([acc[-1]] if len(acc) % 2 else [])
            emit(t0 + r, acc[0])
        return carry

    _loop(t // rows_per_iter, body, inline)


SCAN_GROUP = 4


def _lru_kernel(pfp, pfc, pfn, pbp, pbc, pbn, cw_ref, cb_ref, wab_ref, bab_ref, lam_ref, h0_ref,
                hf_ref, hb_ref, hl_ref, slab_e0, ext_f0, slab_u0, slab_e1, ext_f1, slab_u1, af, bf, ab, bb, carry,
                *, t, n_t):
    i = pl.program_id(1)
    hf_slab, hb_slab = slab_u0, slab_u1
    pitch_e = t + 2 * HALO + SLAB_PAD
    pitch = t + SLAB_PAD

    @pl.when(i == 0)
    def _():
        carry[...] = h0_ref[...]

    for d, (prv, cur, nxt, a_slab, b_slab, slab_e, ext_f, slab_u) in enumerate((
            (pfp, pfc, pfn, af, bf, slab_e0, ext_f0, slab_u0), (pbp, pbc, pbn, ab, bb, slab_e1, ext_f1, slab_u1))):
        tile = i if d == 0 else n_t - 1 - i
        has_prev, has_next = _edge_masks(tile, n_t)
        _fill_ext_slab(slab_e, t,
                       lambda cs: prv[:, cs].astype(F32) * has_prev,
                       lambda cs: cur[:, cs].astype(F32),
                       lambda cs: nxt[:, cs].astype(F32) * has_next)
        _fold(slab_e, pitch_e, ext_f, t + 2 * HALO, inline=True)

        bias = cb_ref[...]

        def emit(row, tile_val):
            slab_u[_tile_rows(row, pitch)] = tile_val + bias

        _conv_folded(ext_f, cw_ref, K_C, t, emit, inline=True)

        neg_lam = -lam_ref[d:d + 1, :]
        softplus = jnp.maximum(neg_lam, 0.0) + jnp.log1p(jnp.exp(-jnp.abs(neg_lam)))
        half_c8 = (-0.5 * LRU_C) * softplus
        for n in range(LRU_BLOCKS):
            cs = slice(n * LRU_BS, (n + 1) * LRU_BS)
            rs = slice(n * pitch, n * pitch + t)
            u = slab_u[rs, :]
            g = jnp.dot(u.astype(BF16), wab_ref[d, n], preferred_element_type=F32)
            t_r = jnp.tanh(g[:, :LRU_BS] + bab_ref[d:d + 1, cs])
            t_i = jnp.tanh(g[:, LRU_BS:] + bab_ref[2 + d:3 + d, cs])
            log_a = half_c8[:, cs] * t_r + half_c8[:, cs]
            a = jnp.exp(log_a)
            y = jnp.tanh(log_a) * (-1.0 - a * a)
            gain = y * lax.rsqrt(jnp.maximum(y, F32_MIN_NORMAL))
            a_slab[rs, :] = a
            b_slab[rs, :] = gain * ((0.5 * t_i + 0.5) * u)

    def chain(a_slab, b_slab, h_slab, h, positions):
        pa = pb = out = None
        for pos in positions:
            rows = _tile_rows(pos, pitch)
            a = a_slab[rows]
            b = b_slab[rows]
            if pa is None:
                pa, pb = a, b
            else:
                pa, pb = a * pa, a * pb + b
            out = pa * h + pb
            h_slab[rows] = out
        return out

    def step(it, c):
        h_f, h_b = c
        j0 = it * SCAN_GROUP
        h_f = chain(af, bf, hf_slab, h_f, [j0 + g for g in range(SCAN_GROUP)])
        h_b = chain(ab, bb, hb_slab, h_b, [t - 1 - j0 - g for g in range(SCAN_GROUP)])
        return h_f, h_b

    h_f, h_b = lax.fori_loop(0, t // SCAN_GROUP, step, (carry[0], carry[1]), unroll=2)
    carry[0] = h_f
    carry[1] = h_b
    hl_ref[...] = carry[...]
    for n in range(N_CHUNK):
        cs = slice(n * LANES, (n + 1) * LANES)
        rs = slice(n * pitch, n * pitch + t)
        hf_ref[:, cs] = hf_slab[rs, :].astype(BF16)
        hb_ref[:, cs] = hb_slab[rs, :].astype(BF16)


def _lru_call(p, h0, conv_c, conv_c_b, wab, bab, lam, t):
    b, l, _ = p.shape
    n_t = l // t
    rev = lambda i: n_t - 1 - i
    small = lambda shape: pl.BlockSpec(shape, lambda b, i: (0,) * len(shape))
    state = lambda: pl.BlockSpec((None, 2, SUBLANES, LANES), lambda b, i: (b, 0, 0, 0))
    slab = lambda rows: pltpu.VMEM((N_CHUNK * (rows + SLAB_PAD), LANES), F32)
    return pl.pallas_call(
        functools.partial(_lru_kernel, t=t, n_t=n_t),
        grid=(b, n_t),
        in_specs=(_halo_specs(t, n_t, C_RU) + _halo_specs(t, n_t, C_RU, rev) + [
            small((K_C, SUBLANES, LANES)), small((SUBLANES, LANES)),
            small((2, LRU_BLOCKS, LRU_BS, 2 * LRU_BS)), small((4, D)), small((2, D)), state()]),
        out_specs=[
            pl.BlockSpec((None, t, D), lambda b, i: (b, i, 0)),
            pl.BlockSpec((None, t, D), lambda b, i: (b, rev(i), 0)),
            state(),
        ],
        out_shape=[
            jax.ShapeDtypeStruct((b, l, D), BF16),
            jax.ShapeDtypeStruct((b, l, D), BF16),
            jax.ShapeDtypeStruct((b, 2, SUBLANES, LANES), F32),
        ],
        scratch_shapes=[
            slab(t + 2 * HALO), pltpu.VMEM((t + 2 * HALO, SUBLANES, LANES), F32), slab(t),
            slab(t + 2 * HALO), pltpu.VMEM((t + 2 * HALO, SUBLANES, LANES), F32), slab(t),
            slab(t), slab(t), slab(t), slab(t), pltpu.VMEM((2, SUBLANES, LANES), F32),
        ],
        compiler_params=_cparams(("arbitrary", "arbitrary")),
    )(p, p, p, p, p, p, conv_c.reshape(K_C, SUBLANES, LANES), conv_c_b.reshape(SUBLANES, LANES), wab, bab, lam, h0)


def _sink_column(sink_ref, kh, rows_per_head):
    rid = lax.broadcasted_iota(jnp.int32, (GROUP * rows_per_head, 1), 0)
    col = sink_ref[kh * GROUP + GROUP - 1:kh * GROUP + GROUP, 0:1]
    for g in range(GROUP - 2, -1, -1):
        col = jnp.where(rid < (g + 1) * rows_per_head, sink_ref[kh * GROUP + g:kh * GROUP + g + 1, 0:1], col)
    return col


def _qk(q, k):
    return lax.dot_general(q, k, (((1,), (1,)), ((), ())), preferred_element_type=F32)


def _stack_heads(q_ref, rs, kh):
    return jnp.concatenate(
        [q_ref[rs, (kh * GROUP + g) * HEAD_DIM:(kh * GROUP + g + 1) * HEAD_DIM] for g in range(GROUP)], axis=0)


def _attn_kernel(q_ref, k_ref, v_ref, kc_ref, vc_ref, sink_ref, o_ref, kpad, vpad, vcx, *, tq, l):
    i = pl.program_id(1)
    blk = WINDOW
    nqb = tq // blk
    nb = l // blk
    rows = GROUP * blk
    vw = 2 * HEAD_DIM

    @pl.when(i == 0)
    def _():
        kpad[0:blk, :] = jnp.zeros((blk, N_KV * HEAD_DIM), BF16)
        kpad[blk:blk + l, :] = k_ref[...]
        kpad[blk + l:2 * blk + l, :] = jnp.zeros((blk, N_KV * HEAD_DIM), BF16)
        vpad[0:blk, :] = jnp.zeros((blk, N_KV * vw), BF16)
        vpad[blk + l:2 * blk + l, :] = jnp.zeros((blk, N_KV * vw), BF16)
        for kh in range(N_KV):
            vpad[blk:blk + l, kh * vw:kh * vw + HEAD_DIM] = v_ref[:, kh * HEAD_DIM:(kh + 1) * HEAD_DIM]
            vpad[blk:blk + l, kh * vw + HEAD_DIM:(kh + 1) * vw] = jnp.ones((l, HEAD_DIM), BF16)
            vcx[:, kh * vw:kh * vw + HEAD_DIM] = vc_ref[:, kh * HEAD_DIM:(kh + 1) * HEAD_DIM]
            vcx[:, kh * vw + HEAD_DIM:(kh + 1) * vw] = jnp.ones((vc_ref.shape[0], HEAD_DIM), BF16)

    qrow = lax.broadcasted_iota(jnp.int32, (rows, blk), 0) & (blk - 1)
    kcol = lax.broadcasted_iota(jnp.int32, (rows, blk), 1)
    bias_prev = jnp.where(kcol >= qrow, 0.0, NEG_INF).astype(F32)
    bias_next = jnp.where(kcol <= qrow, 0.0, NEG_INF).astype(F32)

    units = [(qb, kh) for qb in range(nqb) for kh in range(N_KV)]

    def scores(qb, kh):
        n = i * nqb + qb
        start = pl.multiple_of(n * blk, blk)
        ks = slice(kh * HEAD_DIM, (kh + 1) * HEAD_DIM)
        q4 = _stack_heads(q_ref, slice(qb * blk, (qb + 1) * blk), kh)
        s_loc = _qk(q4, kpad[pl.ds(start, 3 * blk), ks])
        s_ctx = _qk(q4, kc_ref[:, ks])
        no_prev = jnp.where(n > 0, 0.0, NEG_INF).astype(F32)
        no_next = jnp.where(n < nb - 1, 0.0, NEG_INF).astype(F32)
        return (s_loc[:, 0:blk] + (bias_prev + no_prev), s_loc[:, blk:2 * blk],
                s_loc[:, 2 * blk:3 * blk] + (bias_next + no_next), s_ctx)

    def finish(qb, kh, s_prev, s_own, s_next, s_ctx):
        n = i * nqb + qb
        start = pl.multiple_of(n * blk, blk)
        s_sink = _sink_column(sink_ref, kh, blk)
        m_el = jnp.maximum(jnp.maximum(s_prev, s_own), s_next)
        for c in range(s_ctx.shape[1] // blk):
            m_el = jnp.maximum(m_el, s_ctx[:, c * blk:(c + 1) * blk])
        m = jnp.maximum(jnp.max(m_el, axis=-1, keepdims=True), s_sink)
        p_loc = jnp.concatenate([jnp.exp(s_prev - m).astype(BF16), jnp.exp(s_own - m).astype(BF16),
                                 jnp.exp(s_next - m).astype(BF16)], axis=1)
        p_ctx = jnp.exp(s_ctx - m).astype(BF16)
        vs = slice(kh * vw, (kh + 1) * vw)
        o = jnp.dot(p_loc, vpad[pl.ds(start, 3 * blk), vs], preferred_element_type=F32)
        o += jnp.dot(p_ctx, vcx[:, vs], preferred_element_type=F32)
        den = o[:, HEAD_DIM:vw] + jnp.exp(s_sink - m)
        o = o[:, 0:HEAD_DIM] / den
        for g in range(GROUP):
            hs = slice((kh * GROUP + g) * HEAD_DIM, (kh * GROUP + g + 1) * HEAD_DIM)
            o_ref[qb * blk:(qb + 1) * blk, hs] = o[g * blk:(g + 1) * blk, :].astype(BF16)

    pending = scores(*units[0])
    for n_u, unit in enumerate(units):
        nxt = scores(*units[n_u + 1]) if n_u + 1 < len(units) else None
        finish(*unit, *pending)
        pending = nxt


def _attn_call(p, pc, sink_b, tq):
    b, l, _ = p.shape
    lc = pc.shape[1]
    kvw = N_KV * HEAD_DIM
    return pl.pallas_call(
        functools.partial(_attn_kernel, tq=tq, l=l),
        grid=(b, l // tq),
        in_specs=[
            _pcol(tq, C_Q),
            pl.BlockSpec((None, l, kvw), lambda b, i: (b, 0, C_K)),
            pl.BlockSpec((None, l, kvw), lambda b, i: (b, 0, C_V)),
            pl.BlockSpec((None, lc, kvw), lambda b, i: (b, 0, C_K)),
            pl.BlockSpec((None, lc, kvw), lambda b, i: (b, 0, C_V)),
            pl.BlockSpec((N_HEADS, LANES), lambda b, i: (0, 0)),
        ],
        out_specs=pl.BlockSpec((None, tq, D), lambda b, i: (b, i, 0)),
        out_shape=jax.ShapeDtypeStruct((b, l, D), BF16),
        scratch_shapes=[pltpu.VMEM((l + 2 * WINDOW, kvw), BF16), pltpu.VMEM((l + 2 * WINDOW, 2 * kvw), BF16),
                        pltpu.VMEM((lc, 2 * kvw), BF16)],
        compiler_params=_cparams(("arbitrary", "arbitrary")),
    )(p, p, p, pc, pc, sink_b)


def _cattn_kernel(q_ref, kc_ref, vc_ref, sink_ref, o_ref, *, lc):
    for kh in range(N_KV):
        ks = slice(kh * HEAD_DIM, (kh + 1) * HEAD_DIM)
        q4 = _stack_heads(q_ref, slice(0, lc), kh)
        s = _qk(q4, kc_ref[:, ks])
        s_sink = _sink_column(sink_ref, kh, lc)
        m = jnp.maximum(jnp.max(s, axis=-1, keepdims=True), s_sink)
        p = jnp.exp(s - m)
        den = jnp.sum(p, axis=-1, keepdims=True) + jnp.exp(s_sink - m)
        o = jnp.dot(p.astype(BF16), vc_ref[:, ks], preferred_element_type=F32) / den
        for g in range(GROUP):
            hs = slice((kh * GROUP + g) * HEAD_DIM, (kh * GROUP + g + 1) * HEAD_DIM)
            o_ref[:, hs] = o[g * lc:(g + 1) * lc, :].astype(BF16)


def _cattn_call(pc, sink_b):
    b, lc, _ = pc.shape
    kvw = N_KV * HEAD_DIM
    return pl.pallas_call(
        functools.partial(_cattn_kernel, lc=lc),
        grid=(b,),
        in_specs=[
            pl.BlockSpec((None, lc, D), lambda b: (b, 0, C_Q)),
            pl.BlockSpec((None, lc, kvw), lambda b: (b, 0, C_K)),
            pl.BlockSpec((None, lc, kvw), lambda b: (b, 0, C_V)),
            pl.BlockSpec((N_HEADS, LANES), lambda b: (0, 0)),
        ],
        out_specs=pl.BlockSpec((None, lc, D), lambda b: (b, 0, 0)),
        out_shape=jax.ShapeDtypeStruct((b, lc, D), BF16),
        compiler_params=_cparams(("arbitrary",)),
    )(pc, pc, pc, sink_b)


def _bout_kernel(ab_ref, map_, mac, man, rg_ref, mdp, mdc, mdn, ga_ref, gb_ref, gc_ref, gd_ref,
                 hf_ref, hb_ref, yb_ref, x_ref, mod_ref,
                 ca_ref, cd_ref, cdb_ref, lng_ref, lnb_ref, wa_ref, wb_ref, wc_ref, wd_ref, wo_ref,
                 o_ref, *stages, t, n_t):
    stage_a, stage_d = stages[:4], stages[4:]
    i = pl.program_id(1)
    has_prev, has_next = _edge_masks(i, n_t)
    pitch_e = t + 2 * HALO + SLAB_PAD
    pitch = t + SLAB_PAD

    def f32(ref):
        return ref[...].astype(F32)

    def conv(k_taps, w_ref, prv, cur, nxt, stage):
        slab_e, ext_f, slab_z, z_scr = stage
        _fill_ext_slab(slab_e, t,
                       lambda cs: prv[:, cs].astype(F32) * has_prev,
                       lambda cs: cur[:, cs].astype(F32),
                       lambda cs: nxt[:, cs].astype(F32) * has_next)
        _fold(slab_e, pitch_e, ext_f, t + 2 * HALO, inline=True)

        def emit(row, tile_val):
            slab_z[_tile_rows(row, pitch)] = tile_val

        _conv_folded(ext_f, w_ref, k_taps, t, emit, inline=True)
        for c in range(N_CHUNK):
            z_scr[:, c * LANES:(c + 1) * LANES] = slab_z[c * pitch:c * pitch + t, :]
        return z_scr

    def gated(gate_ref, pre, w_ref):
        return f32(gate_ref) * jnp.dot(pre.astype(BF16), w_ref[...], preferred_element_type=F32)

    z_d = conv(K_D, cd_ref, mdp, mdc, mdn, stage_d)
    z_a = conv(K_A, ca_ref, map_, mac, man, stage_a)
    merged = gated(ga_ref, f32(ab_ref) * z_a[...], wa_ref)
    merged += gated(gb_ref, yb_ref[...], wb_ref)
    merged += gated(gc_ref, f32(rg_ref) * (f32(hf_ref) + f32(hb_ref)), wc_ref)
    z = z_d[...] + cdb_ref[...]
    mu = jnp.mean(z, axis=-1, keepdims=True)
    zc = z - mu
    var = jnp.mean(zc * zc, axis=-1, keepdims=True)
    y = zc * lax.rsqrt(var + EPS) * lng_ref[...] + lnb_ref[...]
    merged += gated(gd_ref, y * _sigmoid(y), wd_ref)

    o = jnp.dot(merged.astype(BF16), wo_ref[...], preferred_element_type=F32)
    o_ref[...] = x_ref[...] + mod_ref[2:3, :] * o


def _bout_call(p, hf, hb, yb, x, mods_l, mod_row, conv_a, conv_d, conv_d_b, ln_g, ln_b, wa, wb, wc, wd, wo, layer,
               t):
    b, l, _ = p.shape
    n_t = l // t
    row = lambda: pl.BlockSpec((None, t, D), lambda b, i: (b, i, 0))
    p_specs = ([_pcol(t, C_AB)] + _halo_specs(t, n_t, C_MA) + [_pcol(t, C_RG)] + _halo_specs(t, n_t, C_MD)
               + [_pcol(t, C_GL + k) for k in range(4)])
    in_specs = (p_specs
                + [row(), row(), row(), row(),
                   pl.BlockSpec((None, 6, D), lambda b, i: (mod_row(b), 0, 0)),
                   _resident((K_A, SUBLANES, LANES)), _resident((K_D, SUBLANES, LANES)),
                   _resident((1, D)), _resident((1, D)), _resident((1, D))]
                + [_resident_layer((D, D), layer) for _ in range(5)])
    slab = lambda rows: pltpu.VMEM((N_CHUNK * (rows + SLAB_PAD), LANES), F32)
    return pl.pallas_call(
        functools.partial(_bout_kernel, t=t, n_t=n_t),
        grid=(b, n_t),
        in_specs=in_specs,
        out_specs=pl.BlockSpec((None, t, D), lambda b, i: (b, i, 0)),
        out_shape=jax.ShapeDtypeStruct((b, l, D), F32),
        scratch_shapes=2 * [slab(t + 2 * HALO), pltpu.VMEM((t + 2 * HALO, SUBLANES, LANES), F32), slab(t),
                            pltpu.VMEM((t, D), F32)],
        compiler_params=_cparams(("arbitrary", "arbitrary")),
    )(*([p] * len(p_specs)), hf, hb, yb, x, mods_l, conv_a.reshape(K_A, SUBLANES, LANES),
      conv_d.reshape(K_D, SUBLANES, LANES), conv_d_b, ln_g, ln_b, wa, wb, wc, wd, wo)


FF_CHUNK = 256


def _ffn_kernel(x_ref, mod_ref, g_ref, wi_ref, wo_ref, gf_ref, o_ref, acc, *, final):
    x = x_ref[...]
    h = _modnorm(x, g_ref[...], mod_ref[3:4, :], mod_ref[4:5, :]).astype(BF16)
    for c in range(D_FF // FF_CHUNK):
        gate = jnp.dot(h, wi_ref[:, c * FF_CHUNK:(c + 1) * FF_CHUNK], preferred_element_type=F32)
        up = jnp.dot(h, wi_ref[:, D_FF + c * FF_CHUNK:D_FF + (c + 1) * FF_CHUNK], preferred_element_type=F32)
        act = (gate * _sigmoid(gate) * up).astype(BF16)
        part = jnp.dot(act, wo_ref[c * FF_CHUNK:(c + 1) * FF_CHUNK, :], preferred_element_type=F32)
        if c == 0:
            acc[...] = part
        else:
            acc[...] += part
    y = x + mod_ref[5:6, :] * acc[...]
    if final:
        y = y * lax.rsqrt(jnp.mean(y * y, axis=-1, keepdims=True) + EPS) * gf_ref[...]
    o_ref[...] = y


def _ffn_call(x, mods_l, mod_row, g, wi, wo, layer, g_final, final, t):
    b, l, _ = x.shape
    return pl.pallas_call(
        functools.partial(_ffn_kernel, final=final),
        grid=(b, l // t),
        in_specs=[
            pl.BlockSpec((None, t, D), lambda b, i: (b, i, 0)),
            pl.BlockSpec((None, 6, D), lambda b, i: (mod_row(b), 0, 0)),
            _resident((1, D)), _resident_layer((D, 2 * D_FF), layer), _resident_layer((D_FF, D), layer),
            _resident((1, D)),
        ],
        out_specs=pl.BlockSpec((None, t, D), lambda b, i: (b, i, 0)),
        out_shape=jax.ShapeDtypeStruct((b, l, D), F32),
        scratch_shapes=[pltpu.VMEM((t, D), F32)],
        compiler_params=_cparams(("arbitrary", "arbitrary")),
    )(x, mods_l, g, wi, wo, g_final)


def _rope_table(n_tok):
    rows = n_tok // GRID_W
    half = HEAD_DIM // 2
    inv_freq = ROPE_THETA ** (-jnp.arange(0, half, 2, dtype=F32) / half)
    ang_r = jnp.arange(rows, dtype=F32)[:, None] * inv_freq[None, :]
    ang_c = jnp.arange(GRID_W, dtype=F32)[:, None] * inv_freq[None, :]

    def table(f):
        r = jnp.broadcast_to(f(ang_r)[:, None, :], (rows, GRID_W, half // 2))
        c = jnp.broadcast_to(f(ang_c)[None, :, :], (rows, GRID_W, half // 2))
        return jnp.concatenate([r, r, c, c], axis=-1).reshape(n_tok, HEAD_DIM)

    cos, sin = table(jnp.cos), table(jnp.sin)
    first = (jnp.arange(HEAD_DIM) % half) < half // 2
    return jnp.concatenate([cos, jnp.where(first, -sin, 0.0), jnp.where(first, 0.0, sin)], axis=-1)


def _seq_tile(l, want):
    return min(l, want)


def kernel(x, c, ctx, c_ctx, w_mod, b_mod, g_mix, w_in, conv_a, sink, conv_c, conv_c_b, lru_wa, lru_ba, lru_wx,
           lru_bx, lru_lam, conv_d, conv_d_b, ln_d_g, ln_d_b, w_a_out, w_b_out, w_c_out, w_d_out, w_out, g_ffn,
           w_ffn_in, w_ffn_out, g_final):
    b, l, _ = x.shape
    lc = ctx.shape[1]
    ctx_row = b
    cc = jnp.zeros((8, D), F32).at[:b].set(c).at[ctx_row].set(c_ctx)
    mods = _mod_call(cc, w_mod, b_mod).reshape(DEPTH, 8, 6, D)
    tab = _rope_table(l)
    lat_row = lambda bi: bi
    ctx_rowf = lambda bi: ctx_row
    g_fin = g_final.reshape(1, D)

    w_in_bf = w_in.astype(BF16)
    outs = [w.astype(BF16) for w in (w_a_out, w_b_out, w_c_out, w_d_out, w_out)]
    wi = w_ffn_in.astype(BF16)
    wo = w_ffn_out.astype(BF16)
    xc = ctx
    for li in range(DEPTH):
        need_ctx = li < DEPTH - 1
        wab = (0.5 * jnp.concatenate([lru_wa[li], lru_wx[li]], axis=-1)).astype(BF16)
        bab = 0.5 * jnp.concatenate([lru_ba[li], lru_bx[li]], axis=0)
        sink_b = jnp.broadcast_to(sink[li][:, None], (N_HEADS, LANES))
        g1 = g_mix[li].reshape(1, D)
        g2 = g_ffn[li].reshape(1, D)
        bout_consts = (conv_a[li], conv_d[li], conv_d_b[li].reshape(1, D), ln_d_g[li].reshape(1, D),
                       ln_d_b[li].reshape(1, D))
        ml = mods[li]
        lru_consts = (conv_c[li], conv_c_b[li], wab, bab, lru_lam[li])

        pc = _inproj_call(xc.reshape(1, b * lc, D), ml, ctx_rowf, g1, None, w_in_bf, li, b * lc).reshape(b, lc, N_IN)
        hcf, hcb, hcl = _lru_call(pc, jnp.zeros((b, 2, SUBLANES, LANES), F32), *lru_consts, lc)
        p = _inproj_call(x, ml, lat_row, g1, tab, w_in_bf, li, _seq_tile(l, 1024))
        hf, hb, _ = _lru_call(p, hcl, *lru_consts, _seq_tile(l, 512))
        yb = _attn_call(p, pc, sink_b, _seq_tile(l, 512))
        x = _bout_call(p, hf, hb, yb, x, ml, lat_row, *bout_consts, *outs, li, _seq_tile(l, 256))
        x = _ffn_call(x, ml, lat_row, g2, wi, wo, li, g_fin, li == DEPTH - 1, _seq_tile(l, 512))
        if need_ctx:
            ybc = _cattn_call(pc, sink_b)
            xc = _bout_call(pc, hcf, hcb, ybc, xc, ml, ctx_rowf, *bout_consts, *outs, li, lc)
            xc = _ffn_call(xc.reshape(1, b * lc, D), ml, ctx_rowf, g2, wi, wo, li, g_fin, False,
                           _seq_tile(b * lc, 512)).reshape(b, lc, D)
    return x
```

```python
import functools

import jax
import jax.numpy as jnp
from jax import lax
from jax.experimental import pallas as pl
from jax.experimental.pallas import tpu as pltpu

F32 = jnp.float32
BF16 = jnp.bfloat16

D = 1024
DEPTH = 4
GRID_W = 64
EPS = 1e-6
NEG_INF = -1e30
F32_MIN_NORMAL = 1.17549435e-38
K_A = 3
N_HEADS = 8
N_KV = 2
HEAD_DIM = 128
GROUP = N_HEADS // N_KV
WINDOW = 128
ROPE_THETA = 10000.0
K_C = 4
LRU_BLOCKS = 8
LRU_BS = D // LRU_BLOCKS
LRU_C = 8.0
K_D = 31
D_FF = 2816
N_IN = 12800

C_MA, C_AB, C_MD, C_RG, C_RU, C_GL, C_Q = 0, 2, 3, 5, 6, 7, 11
C_K, C_V = 48, 49
HALO = 16
SUBLANES = 8
LANES = 128
N_CHUNK = D // LANES
SLAB_PAD = 8
VMEM_LIMIT = 56 * 1024 * 1024


def _cparams(sem):
    return pltpu.CompilerParams(dimension_semantics=sem, vmem_limit_bytes=VMEM_LIMIT)


def _resident(shape):
    nd = len(shape)
    return pl.BlockSpec(shape, lambda *_: (0,) * nd, pipeline_mode=pl.Buffered(1))


def _resident_layer(shape, layer):
    nd = len(shape)
    return pl.BlockSpec((None,) + shape, lambda *_: (layer,) + (0,) * nd, pipeline_mode=pl.Buffered(1))


def _modnorm(x, g, shift, scale):
    y = x * lax.rsqrt(jnp.mean(x * x, axis=-1, keepdims=True) + EPS)
    return (y * g) * (1.0 + scale) + shift


def _sigmoid(x):
    return 0.5 * jnp.tanh(0.5 * x) + 0.5


def _mod_kernel(c_ref, w_ref, b_ref, o_ref):
    s = c_ref[...]
    s = s * _sigmoid(s)
    w = w_ref[...]
    s_hi = s.astype(BF16)
    s_lo = (s - s_hi.astype(F32)).astype(BF16)
    w_hi = w.astype(BF16)
    w_lo = (w - w_hi.astype(F32)).astype(BF16)
    acc = jnp.dot(s_hi, w_hi, preferred_element_type=F32)
    acc += jnp.dot(s_lo, w_hi, preferred_element_type=F32)
    acc += jnp.dot(s_hi, w_lo, preferred_element_type=F32)
    o_ref[...] = acc + b_ref[...]


def _mod_call(cc, w_mod, b_mod):
    tn = 1536
    return pl.pallas_call(
        _mod_kernel,
        grid=(DEPTH, 6 * D // tn),
        in_specs=[
            pl.BlockSpec((8, D), lambda l, j: (0, 0)),
            pl.BlockSpec((None, D, tn), lambda l, j: (l, 0, j)),
            pl.BlockSpec((None, 1, tn), lambda l, j: (l, 0, j)),
        ],
        out_specs=pl.BlockSpec((None, 8, tn), lambda l, j: (l, 0, j)),
        out_shape=jax.ShapeDtypeStruct((DEPTH, 8, 6 * D), F32),
        compiler_params=_cparams(("arbitrary", "arbitrary")),
    )(cc, w_mod, b_mod.reshape(DEPTH, 1, 6 * D))


IN_SUB = 512
IN_GROUP = 5
IN_TN = IN_SUB * IN_GROUP
Q_SCALE = HEAD_DIM ** -0.5
IN_PERM = (2, 4, 3, 5, 0, 1, 13, 15, 14, 16, 9, 10, 11, 12, 17, 18, 19, 20, 21, 22, 23, 24, 6, 7, 8)


def _rope(t, tab_ref):
    cos = tab_ref[:, 0:LANES]
    sin_lo = tab_ref[:, LANES:2 * LANES]
    sin_hi = tab_ref[:, 2 * LANES:3 * LANES]
    return t * cos + pltpu.roll(t, 96, 1) * sin_lo + pltpu.roll(t, 32, 1) * sin_hi


def _inproj_kernel(*refs, rope):
    x_ref, mod_ref, g_ref = refs[:3]
    tab_ref = refs[3] if rope else None
    w_refs = refs[-(IN_GROUP + 2):-2]
    o_ref, h_ref = refs[-2:]
    j = pl.program_id(2)

    @pl.when(j == 0)
    def _():
        h_ref[...] = _modnorm(x_ref[...], g_ref[...], mod_ref[0:1, :], mod_ref[1:2, :]).astype(BF16)

    def mm(s):
        return jnp.dot(h_ref[...], w_refs[s][...], preferred_element_type=F32)

    def put(pos, val):
        o_ref[:, pos * IN_SUB:(pos + 1) * IN_SUB] = val.astype(BF16)

    def zero(pos):
        o_ref[:, pos * IN_SUB:(pos + 1) * IN_SUB] = jnp.zeros((o_ref.shape[0], IN_SUB), BF16)

    def rotated(acc, n_rot, scale):
        for c in range(IN_SUB // LANES):
            t = acc[:, c * LANES:(c + 1) * LANES]
            if rope and c < n_rot:
                t = _rope(t, tab_ref)
            if scale != 1.0:
                t = t * scale
            yield c, t

    @pl.when(j == 0)
    def _():
        put(0, mm(0) * mm(1))
        put(1, mm(2) * mm(3))
        zero(2)
        zero(3)
        put(4, mm(4))

    @pl.when(j == 1)
    def _():
        put(0, mm(0))
        put(1, mm(1) * _sigmoid(mm(2)))
        put(2, mm(3) * _sigmoid(mm(4)))
        zero(3)
        zero(4)

    @pl.when(j == 2)
    def _():
        put(0, jax.nn.gelu(mm(0)))
        put(1, jax.nn.gelu(mm(1)))
        put(2, mm(2))
        put(3, mm(3))
        put(4, _sigmoid(mm(4)))

    @pl.when(j == 3)
    def _():
        for s in range(IN_GROUP):
            put(s, _sigmoid(mm(s)))

    @pl.when(j == 4)
    def _():
        put(0, _sigmoid(mm(0)))
        put(1, _sigmoid(mm(1)))
        n_chunk = IN_SUB // LANES
        for s, n_rot, scale in ((2, n_chunk, Q_SCALE), (3, n_chunk, Q_SCALE), (4, n_chunk // 2, 1.0)):
            for c, t in rotated(mm(s), n_rot, scale):
                o_ref[:, s * IN_SUB + c * LANES:s * IN_SUB + (c + 1) * LANES] = t.astype(BF16)


def _w_in_subtile(j, s):
    idx = IN_PERM[s]
    for grp in range(1, N_IN // IN_TN):
        idx = jnp.where(j == grp, IN_PERM[grp * IN_GROUP + s], idx)
    return idx


def _inproj_call(x, mods_l, mod_row, g, tab, w_all, layer, tm):
    bx, lx, _ = x.shape
    rope = tab is not None
    in_specs = [
        pl.BlockSpec((None, tm, D), lambda b, i, j: (b, i, 0)),
        pl.BlockSpec((None, 6, D), lambda b, i, j: (mod_row(b), 0, 0)),
        pl.BlockSpec((1, D), lambda b, i, j: (0, 0)),
    ]
    args = [x, mods_l, g]
    if rope:
        in_specs.append(pl.BlockSpec((tm, 3 * LANES), lambda b, i, j: (i, 0)))
        args.append(tab)
    for s in range(IN_GROUP):
        in_specs.append(pl.BlockSpec((None, D, IN_SUB), lambda b, i, j, s=s: (layer, 0, _w_in_subtile(j, s))))
        args.append(w_all)
    return pl.pallas_call(
        functools.partial(_inproj_kernel, rope=rope),
        grid=(bx, lx // tm, N_IN // IN_TN),
        in_specs=in_specs,
        out_specs=pl.BlockSpec((None, tm, IN_TN), lambda b, i, j: (b, i, j)),
        out_shape=jax.ShapeDtypeStruct((bx, lx, N_IN), BF16),
        scratch_shapes=[pltpu.VMEM((tm, D), BF16)],
        compiler_params=_cparams(("arbitrary", "arbitrary", "arbitrary")),
    )(*args)


def _pcol(t, c, width=D):
    return pl.BlockSpec((None, t, width), lambda b, i: (b, i, c))


def _halo_specs(t, n_t, c, tile_of=lambda i: i):
    per = t // HALO
    last = n_t * per - 1
    return [
        pl.BlockSpec((None, HALO, D), lambda b, i: (b, jnp.maximum(tile_of(i) * per - 1, 0), c)),
        pl.BlockSpec((None, t, D), lambda b, i: (b, tile_of(i), c)),
        pl.BlockSpec((None, HALO, D), lambda b, i: (b, jnp.minimum((tile_of(i) + 1) * per, last), c)),
    ]


def _edge_masks(tile, n_t):
    has_prev = jnp.where(tile > 0, 1.0, 0.0).astype(F32)
    has_next = jnp.where(tile < n_t - 1, 1.0, 0.0).astype(F32)
    return has_prev, has_next


def _tile_rows(j, pitch):
    return (pl.ds(j, N_CHUNK, stride=pitch), slice(None))


def _fill_ext_slab(slab, t, prev, cur, nxt):
    pitch = t + 2 * HALO + SLAB_PAD
    for c in range(N_CHUNK):
        cs = slice(c * LANES, (c + 1) * LANES)
        slab[c * pitch:c * pitch + HALO, :] = prev(cs)
        slab[c * pitch + HALO:c * pitch + HALO + t, :] = cur(cs)
        slab[c * pitch + HALO + t:c * pitch + 2 * HALO + t, :] = nxt(cs)


def _loop(n, body, inline, unroll=1):
    if inline:
        for i in range(n):
            body(i, 0)
    else:
        lax.fori_loop(0, n, body, 0, unroll=unroll)


def _fold(slab, pitch, dst, n_rows, inline=False):
    def body(r, carry):
        dst[r] = slab[_tile_rows(r, pitch)]
        return carry

    _loop(n_rows, body, inline, unroll=8)


def _conv_folded(ext_f, w_ref, k_taps, t, emit, rows_per_iter=4, inline=False):
    off = HALO - k_taps // 2

    def body(it, carry):
        t0 = it * rows_per_iter
        loaded = {}

        def x(r):
            if r not in loaded:
                loaded[r] = ext_f[t0 + (off + r)]
            return loaded[r]

        n_part = min(4, k_taps)
        parts = [[None] * n_part for _ in range(rows_per_iter)]
        for k in range(k_taps):
            w = w_ref[k]
            for r in range(rows_per_iter):
                term = w * x(k + r)
                prev = parts[r][k % n_part]
                parts[r][k % n_part] = term if prev is None else prev + term
        for r in range(rows_per_iter):
            acc = parts[r]
            while len(acc) > 1:
                acc = [acc[i] + acc[i + 1] for i in range(0, len(acc) - 1, 2)] + ([acc[-1]] if len(acc) % 2 else [])
            emit(t0 + r, acc[0])
        return carry

    _loop(t // rows_per_iter, body, inline)


SCAN_GROUP = 4


def _lru_kernel(pfp, pfc, pfn, pbp, pbc, pbn, cw_ref, cb_ref, wab_ref, bab_ref, lam_ref, h0_ref,
                hf_ref, hb_ref, hl_ref, slab_e0, ext_f0, slab_u0, slab_e1, ext_f1, slab_u1, af, bf, ab, bb, carry,
                *, t, n_t):
    i = pl.program_id(1)
    hf_slab, hb_slab = slab_u0, slab_u1
    pitch_e = t + 2 * HALO + SLAB_PAD
    pitch = t + SLAB_PAD

    @pl.when(i == 0)
    def _():
        carry[...] = h0_ref[...]

    for d, (prv, cur, nxt, a_slab, b_slab, slab_e, ext_f, slab_u) in enumerate((
            (pfp, pfc, pfn, af, bf, slab_e0, ext_f0, slab_u0), (pbp, pbc, pbn, ab, bb, slab_e1, ext_f1, slab_u1))):
        tile = i if d == 0 else n_t - 1 - i
        has_prev, has_next = _edge_masks(tile, n_t)
        _fill_ext_slab(slab_e, t,
                       lambda cs: prv[:, cs].astype(F32) * has_prev,
                       lambda cs: cur[:, cs].astype(F32),
                       lambda cs: nxt[:, cs].astype(F32) * has_next)
        _fold(slab_e, pitch_e, ext_f, t + 2 * HALO, inline=True)

        bias = cb_ref[...]

        def emit(row, tile_val):
            slab_u[_tile_rows(row, pitch)] = tile_val + bias

        _conv_folded(ext_f, cw_ref, K_C, t, emit, inline=True)

        neg_lam = -lam_ref[d:d + 1, :]
        softplus = jnp.maximum(neg_lam, 0.0) + jnp.log1p(jnp.exp(-jnp.abs(neg_lam)))
        half_c8 = (-0.5 * LRU_C) * softplus
        for n in range(LRU_BLOCKS):
            cs = slice(n * LRU_BS, (n + 1) * LRU_BS)
            rs = slice(n * pitch, n * pitch + t)
            u = slab_u[rs, :]
            g = jnp.dot(u.astype(BF16), wab_ref[d, n], preferred_element_type=F32)
            t_r = jnp.tanh(g[:, :LRU_BS] + bab_ref[d:d + 1, cs])
            t_i = jnp.tanh(g[:, LRU_BS:] + bab_ref[2 + d:3 + d, cs])
            log_a = half_c8[:, cs] * t_r + half_c8[:, cs]
            a = jnp.exp(log_a)
            y = jnp.tanh(log_a) * (-1.0 - a * a)
            gain = y * lax.rsqrt(jnp.maximum(y, F32_MIN_NORMAL))
            a_slab[rs, :] = a
            b_slab[rs, :] = gain * ((0.5 * t_i + 0.5) * u)

    def chain(a_slab, b_slab, h_slab, h, positions):
        pa = pb = out = None
        for pos in positions:
            rows = _tile_rows(pos, pitch)
            a = a_slab[rows]
            b = b_slab[rows]
            if pa is None:
                pa, pb = a, b
            else:
                pa, pb = a * pa, a * pb + b
            out = pa * h + pb
            h_slab[rows] = out
        return out

    def step(it, c):
        h_f, h_b = c
        j0 = it * SCAN_GROUP
        h_f = chain(af, bf, hf_slab, h_f, [j0 + g for g in range(SCAN_GROUP)])
        h_b = chain(ab, bb, hb_slab, h_b, [t - 1 - j0 - g for g in range(SCAN_GROUP)])
        return h_f, h_b

    h_f, h_b = lax.fori_loop(0, t // SCAN_GROUP, step, (carry[0], carry[1]), unroll=2)
    carry[0] = h_f
    carry[1] = h_b
    hl_ref[...] = carry[...]
    for n in range(N_CHUNK):
        cs = slice(n * LANES, (n + 1) * LANES)
        rs = slice(n * pitch, n * pitch + t)
        hf_ref[:, cs] = hf_slab[rs, :].astype(BF16)
        hb_ref[:, cs] = hb_slab[rs, :].astype(BF16)


def _lru_call(p, h0, conv_c, conv_c_b, wab, bab, lam, t):
    b, l, _ = p.shape
    n_t = l // t
    rev = lambda i: n_t - 1 - i
    small = lambda shape: pl.BlockSpec(shape, lambda b, i: (0,) * len(shape))
    state = lambda: pl.BlockSpec((None, 2, SUBLANES, LANES), lambda b, i: (b, 0, 0, 0))
    slab = lambda rows: pltpu.VMEM((N_CHUNK * (rows + SLAB_PAD), LANES), F32)
    return pl.pallas_call(
        functools.partial(_lru_kernel, t=t, n_t=n_t),
        grid=(b, n_t),
        in_specs=(_halo_specs(t, n_t, C_RU) + _halo_specs(t, n_t, C_RU, rev) + [
            small((K_C, SUBLANES, LANES)), small((SUBLANES, LANES)),
            small((2, LRU_BLOCKS, LRU_BS, 2 * LRU_BS)), small((4, D)), small((2, D)), state()]),
        out_specs=[
            pl.BlockSpec((None, t, D), lambda b, i: (b, i, 0)),
            pl.BlockSpec((None, t, D), lambda b, i: (b, rev(i), 0)),
            state(),
        ],
        out_shape=[
            jax.ShapeDtypeStruct((b, l, D), BF16),
            jax.ShapeDtypeStruct((b, l, D), BF16),
            jax.ShapeDtypeStruct((b, 2, SUBLANES, LANES), F32),
        ],
        scratch_shapes=[
            slab(t + 2 * HALO), pltpu.VMEM((t + 2 * HALO, SUBLANES, LANES), F32), slab(t),
            slab(t + 2 * HALO), pltpu.VMEM((t + 2 * HALO, SUBLANES, LANES), F32), slab(t),
            slab(t), slab(t), slab(t), slab(t), pltpu.VMEM((2, SUBLANES, LANES), F32),
        ],
        compiler_params=_cparams(("arbitrary", "arbitrary")),
    )(p, p, p, p, p, p, conv_c.reshape(K_C, SUBLANES, LANES), conv_c_b.reshape(SUBLANES, LANES), wab, bab, lam, h0)


def _sink_column(sink_ref, kh, rows_per_head):
    rid = lax.broadcasted_iota(jnp.int32, (GROUP * rows_per_head, 1), 0)
    col = sink_ref[kh * GROUP + GROUP - 1:kh * GROUP + GROUP, 0:1]
    for g in range(GROUP - 2, -1, -1):
        col = jnp.where(rid < (g + 1) * rows_per_head, sink_ref[kh * GROUP + g:kh * GROUP + g + 1, 0:1], col)
    return col


def _qk(q, k):
    return lax.dot_general(q, k, (((1,), (1,)), ((), ())), preferred_element_type=F32)


def _stack_heads(q_ref, rs, kh):
    return jnp.concatenate(
        [q_ref[rs, (kh * GROUP + g) * HEAD_DIM:(kh * GROUP + g + 1) * HEAD_DIM] for g in range(GROUP)], axis=0)


def _attn_kernel(q_ref, k_ref, v_ref, kc_ref, vc_ref, sink_ref, o_ref, kpad, vpad, vcx, *, tq, l):
    i = pl.program_id(1)
    blk = WINDOW
    nqb = tq // blk
    nb = l // blk
    rows = GROUP * blk
    vw = 2 * HEAD_DIM

    @pl.when(i == 0)
    def _():
        kpad[0:blk, :] = jnp.zeros((blk, N_KV * HEAD_DIM), BF16)
        kpad[blk:blk + l, :] = k_ref[...]
        kpad[blk + l:2 * blk + l, :] = jnp.zeros((blk, N_KV * HEAD_DIM), BF16)
        vpad[0:blk, :] = jnp.zeros((blk, N_KV * vw), BF16)
        vpad[blk + l:2 * blk + l, :] = jnp.zeros((blk, N_KV * vw), BF16)
        for kh in range(N_KV):
            vpad[blk:blk + l, kh * vw:kh * vw + HEAD_DIM] = v_ref[:, kh * HEAD_DIM:(kh + 1) * HEAD_DIM]
            vpad[blk:blk + l, kh * vw + HEAD_DIM:(kh + 1) * vw] = jnp.ones((l, HEAD_DIM), BF16)
            vcx[:, kh * vw:kh * vw + HEAD_DIM] = vc_ref[:, kh * HEAD_DIM:(kh + 1) * HEAD_DIM]
            vcx[:, kh * vw + HEAD_DIM:(kh + 1) * vw] = jnp.ones((vc_ref.shape[0], HEAD_DIM), BF16)

    qrow = lax.broadcasted_iota(jnp.int32, (rows, blk), 0) & (blk - 1)
    kcol = lax.broadcasted_iota(jnp.int32, (rows, blk), 1)
    bias_prev = jnp.where(kcol >= qrow, 0.0, NEG_INF).astype(F32)
    bias_next = jnp.where(kcol <= qrow, 0.0, NEG_INF).astype(F32)

    units = [(qb, kh) for qb in range(nqb) for kh in range(N_KV)]

    def scores(qb, kh):
        n = i * nqb + qb
        start = pl.multiple_of(n * blk, blk)
        ks = slice(kh * HEAD_DIM, (kh + 1) * HEAD_DIM)
        q4 = _stack_heads(q_ref, slice(qb * blk, (qb + 1) * blk), kh)
        s_loc = _qk(q4, kpad[pl.ds(start, 3 * blk), ks])
        s_ctx = _qk(q4, kc_ref[:, ks])
        no_prev = jnp.where(n > 0, 0.0, NEG_INF).astype(F32)
        no_next = jnp.where(n < nb - 1, 0.0, NEG_INF).astype(F32)
        return (s_loc[:, 0:blk] + (bias_prev + no_prev), s_loc[:, blk:2 * blk],
                s_loc[:, 2 * blk:3 * blk] + (bias_next + no_next), s_ctx)

    def finish(qb, kh, s_prev, s_own, s_next, s_ctx):
        n = i * nqb + qb
        start = pl.multiple_of(n * blk, blk)
        s_sink = _sink_column(sink_ref, kh, blk)
        m_el = jnp.maximum(jnp.maximum(s_prev, s_own), s_next)
        for c in range(s_ctx.shape[1] // blk):
            m_el = jnp.maximum(m_el, s_ctx[:, c * blk:(c + 1) * blk])
        m = jnp.maximum(jnp.max(m_el, axis=-1, keepdims=True), s_sink)
        p_loc = jnp.concatenate([jnp.exp(s_prev - m).astype(BF16), jnp.exp(s_own - m).astype(BF16),
                                 jnp.exp(s_next - m).astype(BF16)], axis=1)
        p_ctx = jnp.exp(s_ctx - m).astype(BF16)
        vs = slice(kh * vw, (kh + 1) * vw)
        o = jnp.dot(p_loc, vpad[pl.ds(start, 3 * blk), vs], preferred_element_type=F32)
        o += jnp.dot(p_ctx, vcx[:, vs], preferred_element_type=F32)
        den = o[:, HEAD_DIM:vw] + jnp.exp(s_sink - m)
        o = o[:, 0:HEAD_DIM] / den
        for g in range(GROUP):
            hs = slice((kh * GROUP + g) * HEAD_DIM, (kh * GROUP + g + 1) * HEAD_DIM)
            o_ref[qb * blk:(qb + 1) * blk, hs] = o[g * blk:(g + 1) * blk, :].astype(BF16)

    pending = scores(*units[0])
    for n_u, unit in enumerate(units):
        nxt = scores(*units[n_u + 1]) if n_u + 1 < len(units) else None
        finish(*unit, *pending)
        pending = nxt


def _attn_call(p, pc, sink_b, tq):
    b, l, _ = p.shape
    lc = pc.shape[1]
    kvw = N_KV * HEAD_DIM
    return pl.pallas_call(
        functools.partial(_attn_kernel, tq=tq, l=l),
        grid=(b, l // tq),
        in_specs=[
            _pcol(tq, C_Q),
            pl.BlockSpec((None, l, kvw), lambda b, i: (b, 0, C_K)),
            pl.BlockSpec((None, l, kvw), lambda b, i: (b, 0, C_V)),
            pl.BlockSpec((None, lc, kvw), lambda b, i: (b, 0, C_K)),
            pl.BlockSpec((None, lc, kvw), lambda b, i: (b, 0, C_V)),
            pl.BlockSpec((N_HEADS, LANES), lambda b, i: (0, 0)),
        ],
        out_specs=pl.BlockSpec((None, tq, D), lambda b, i: (b, i, 0)),
        out_shape=jax.ShapeDtypeStruct((b, l, D), BF16),
        scratch_shapes=[pltpu.VMEM((l + 2 * WINDOW, kvw), BF16), pltpu.VMEM((l + 2 * WINDOW, 2 * kvw), BF16),
                        pltpu.VMEM((lc, 2 * kvw), BF16)],
        compiler_params=_cparams(("arbitrary", "arbitrary")),
    )(p, p, p, pc, pc, sink_b)


def _cattn_kernel(q_ref, kc_ref, vc_ref, sink_ref, o_ref, *, lc):
    for kh in range(N_KV):
        ks = slice(kh * HEAD_DIM, (kh + 1) * HEAD_DIM)
        q4 = _stack_heads(q_ref, slice(0, lc), kh)
        s = _qk(q4, kc_ref[:, ks])
        s_sink = _sink_column(sink_ref, kh, lc)
        m = jnp.maximum(jnp.max(s, axis=-1, keepdims=True), s_sink)
        p = jnp.exp(s - m)
        den = jnp.sum(p, axis=-1, keepdims=True) + jnp.exp(s_sink - m)
        o = jnp.dot(p.astype(BF16), vc_ref[:, ks], preferred_element_type=F32) / den
        for g in range(GROUP):
            hs = slice((kh * GROUP + g) * HEAD_DIM, (kh * GROUP + g + 1) * HEAD_DIM)
            o_ref[:, hs] = o[g * lc:(g + 1) * lc, :].astype(BF16)


def _cattn_call(pc, sink_b):
    b, lc, _ = pc.shape
    kvw = N_KV * HEAD_DIM
    return pl.pallas_call(
        functools.partial(_cattn_kernel, lc=lc),
        grid=(b,),
        in_specs=[
            pl.BlockSpec((None, lc, D), lambda b: (b, 0, C_Q)),
            pl.BlockSpec((None, lc, kvw), lambda b: (b, 0, C_K)),
            pl.BlockSpec((None, lc, kvw), lambda b: (b, 0, C_V)),
            pl.BlockSpec((N_HEADS, LANES), lambda b: (0, 0)),
        ],
        out_specs=pl.BlockSpec((None, lc, D), lambda b: (b, 0, 0)),
        out_shape=jax.ShapeDtypeStruct((b, lc, D), BF16),
        compiler_params=_cparams(("arbitrary",)),
    )(pc, pc, pc, sink_b)


def _bout_kernel(ab_ref, map_, mac, man, rg_ref, mdp, mdc, mdn, ga_ref, gb_ref, gc_ref, gd_ref,
                 hf_ref, hb_ref, yb_ref, x_ref, mod_ref,
                 ca_ref, cd_ref, cdb_ref, lng_ref, lnb_ref, wa_ref, wb_ref, wc_ref, wd_ref, wo_ref,
                 o_ref, slab_e, ext_f, slab_z, z_scr, *, t, n_t):
    i = pl.program_id(1)
    has_prev, has_next = _edge_masks(i, n_t)
    pitch_e = t + 2 * HALO + SLAB_PAD
    pitch = t + SLAB_PAD

    def f32(ref):
        return ref[...].astype(F32)

    def conv(k_taps, w_ref, prv, cur, nxt):
        _fill_ext_slab(slab_e, t,
                       lambda cs: prv[:, cs].astype(F32) * has_prev,
                       lambda cs: cur[:, cs].astype(F32),
                       lambda cs: nxt[:, cs].astype(F32) * has_next)
        _fold(slab_e, pitch_e, ext_f, t + 2 * HALO, inline=True)

        def emit(row, tile_val):
            slab_z[_tile_rows(row, pitch)] = tile_val

        _conv_folded(ext_f, w_ref, k_taps, t, emit, inline=True)
        for c in range(N_CHUNK):
            z_scr[:, c * LANES:(c + 1) * LANES] = slab_z[c * pitch:c * pitch + t, :]

    def gated(gate_ref, pre, w_ref):
        return f32(gate_ref) * jnp.dot(pre.astype(BF16), w_ref[...], preferred_element_type=F32)

    conv(K_A, ca_ref, map_, mac, man)
    merged = gated(ga_ref, f32(ab_ref) * z_scr[...], wa_ref)
    merged += gated(gb_ref, yb_ref[...], wb_ref)
    merged += gated(gc_ref, f32(rg_ref) * (f32(hf_ref) + f32(hb_ref)), wc_ref)
    conv(K_D, cd_ref, mdp, mdc, mdn)
    z = z_scr[...] + cdb_ref[...]
    mu = jnp.mean(z, axis=-1, keepdims=True)
    zc = z - mu
    var = jnp.mean(zc * zc, axis=-1, keepdims=True)
    y = zc * lax.rsqrt(var + EPS) * lng_ref[...] + lnb_ref[...]
    merged += gated(gd_ref, y * _sigmoid(y), wd_ref)

    o = jnp.dot(merged.astype(BF16), wo_ref[...], preferred_element_type=F32)
    o_ref[...] = x_ref[...] + mod_ref[2:3, :] * o


def _bout_call(p, hf, hb, yb, x, mods_l, mod_row, conv_a, conv_d, conv_d_b, ln_g, ln_b, wa, wb, wc, wd, wo, layer,
               t):
    b, l, _ = p.shape
    n_t = l // t
    row = lambda: pl.BlockSpec((None, t, D), lambda b, i: (b, i, 0))
    p_specs = ([_pcol(t, C_AB)] + _halo_specs(t, n_t, C_MA) + [_pcol(t, C_RG)] + _halo_specs(t, n_t, C_MD)
               + [_pcol(t, C_GL + k) for k in range(4)])
    in_specs = (p_specs
                + [row(), row(), row(), row(),
                   pl.BlockSpec((None, 6, D), lambda b, i: (mod_row(b), 0, 0)),
                   _resident((K_A, SUBLANES, LANES)), _resident((K_D, SUBLANES, LANES)),
                   _resident((1, D)), _resident((1, D)), _resident((1, D))]
                + [_resident_layer((D, D), layer) for _ in range(5)])
    slab = lambda rows: pltpu.VMEM((N_CHUNK * (rows + SLAB_PAD), LANES), F32)
    return pl.pallas_call(
        functools.partial(_bout_kernel, t=t, n_t=n_t),
        grid=(b, n_t),
        in_specs=in_specs,
        out_specs=pl.BlockSpec((None, t, D), lambda b, i: (b, i, 0)),
        out_shape=jax.ShapeDtypeStruct((b, l, D), F32),
        scratch_shapes=[slab(t + 2 * HALO), pltpu.VMEM((t + 2 * HALO, SUBLANES, LANES), F32), slab(t),
                        pltpu.VMEM((t, D), F32)],
        compiler_params=_cparams(("arbitrary", "arbitrary")),
    )(*([p] * len(p_specs)), hf, hb, yb, x, mods_l, conv_a.reshape(K_A, SUBLANES, LANES),
      conv_d.reshape(K_D, SUBLANES, LANES), conv_d_b, ln_g, ln_b, wa, wb, wc, wd, wo)


FF_CHUNK = 256


def _ffn_kernel(x_ref, mod_ref, g_ref, wi_ref, wo_ref, gf_ref, o_ref, acc, *, final):
    x = x_ref[...]
    h = _modnorm(x, g_ref[...], mod_ref[3:4, :], mod_ref[4:5, :]).astype(BF16)
    for c in range(D_FF // FF_CHUNK):
        gate = jnp.dot(h, wi_ref[:, c * FF_CHUNK:(c + 1) * FF_CHUNK], preferred_element_type=F32)
        up = jnp.dot(h, wi_ref[:, D_FF + c * FF_CHUNK:D_FF + (c + 1) * FF_CHUNK], preferred_element_type=F32)
        act = (gate * _sigmoid(gate) * up).astype(BF16)
        part = jnp.dot(act, wo_ref[c * FF_CHUNK:(c + 1) * FF_CHUNK, :], preferred_element_type=F32)
        if c == 0:
            acc[...] = part
        else:
            acc[...] += part
    y = x + mod_ref[5:6, :] * acc[...]
    if final:
        y = y * lax.rsqrt(jnp.mean(y * y, axis=-1, keepdims=True) + EPS) * gf_ref[...]
    o_ref[...] = y


def _ffn_call(x, mods_l, mod_row, g, wi, wo, layer, g_final, final, t):
    b, l, _ = x.shape
    return pl.pallas_call(
        functools.partial(_ffn_kernel, final=final),
        grid=(b, l // t),
        in_specs=[
            pl.BlockSpec((None, t, D), lambda b, i: (b, i, 0)),
            pl.BlockSpec((None, 6, D), lambda b, i: (mod_row(b), 0, 0)),
            _resident((1, D)), _resident_layer((D, 2 * D_FF), layer), _resident_layer((D_FF, D), layer),
            _resident((1, D)),
        ],
        out_specs=pl.BlockSpec((None, t, D), lambda b, i: (b, i, 0)),
        out_shape=jax.ShapeDtypeStruct((b, l, D), F32),
        scratch_shapes=[pltpu.VMEM((t, D), F32)],
        compiler_params=_cparams(("arbitrary", "arbitrary")),
    )(x, mods_l, g, wi, wo, g_final)


def _rope_table(n_tok):
    rows = n_tok // GRID_W
    half = HEAD_DIM // 2
    inv_freq = ROPE_THETA ** (-jnp.arange(0, half, 2, dtype=F32) / half)
    ang_r = jnp.arange(rows, dtype=F32)[:, None] * inv_freq[None, :]
    ang_c = jnp.arange(GRID_W, dtype=F32)[:, None] * inv_freq[None, :]

    def table(f):
        r = jnp.broadcast_to(f(ang_r)[:, None, :], (rows, GRID_W, half // 2))
        c = jnp.broadcast_to(f(ang_c)[None, :, :], (rows, GRID_W, half // 2))
        return jnp.concatenate([r, r, c, c], axis=-1).reshape(n_tok, HEAD_DIM)

    cos, sin = table(jnp.cos), table(jnp.sin)
    first = (jnp.arange(HEAD_DIM) % half) < half // 2
    return jnp.concatenate([cos, jnp.where(first, -sin, 0.0), jnp.where(first, 0.0, sin)], axis=-1)


def _seq_tile(l, want):
    return min(l, want)


def kernel(x, c, ctx, c_ctx, w_mod, b_mod, g_mix, w_in, conv_a, sink, conv_c, conv_c_b, lru_wa, lru_ba, lru_wx,
           lru_bx, lru_lam, conv_d, conv_d_b, ln_d_g, ln_d_b, w_a_out, w_b_out, w_c_out, w_d_out, w_out, g_ffn,
           w_ffn_in, w_ffn_out, g_final):
    b, l, _ = x.shape
    lc = ctx.shape[1]
    ctx_row = b
    cc = jnp.zeros((8, D), F32).at[:b].set(c).at[ctx_row].set(c_ctx)
    mods = _mod_call(cc, w_mod, b_mod).reshape(DEPTH, 8, 6, D)
    tab = _rope_table(l)
    lat_row = lambda bi: bi
    ctx_rowf = lambda bi: ctx_row
    g_fin = g_final.reshape(1, D)

    w_in_bf = w_in.astype(BF16)
    outs = [w.astype(BF16) for w in (w_a_out, w_b_out, w_c_out, w_d_out, w_out)]
    wi = w_ffn_in.astype(BF16)
    wo = w_ffn_out.astype(BF16)
    xc = ctx
    for li in range(DEPTH):
        need_ctx = li < DEPTH - 1
        wab = (0.5 * jnp.concatenate([lru_wa[li], lru_wx[li]], axis=-1)).astype(BF16)
        bab = 0.5 * jnp.concatenate([lru_ba[li], lru_bx[li]], axis=0)
        sink_b = jnp.broadcast_to(sink[li][:, None], (N_HEADS, LANES))
        g1 = g_mix[li].reshape(1, D)
        g2 = g_ffn[li].reshape(1, D)
        bout_consts = (conv_a[li], conv_d[li], conv_d_b[li].reshape(1, D), ln_d_g[li].reshape(1, D),
                       ln_d_b[li].reshape(1, D))
        ml = mods[li]
        lru_consts = (conv_c[li], conv_c_b[li], wab, bab, lru_lam[li])

        pc = _inproj_call(xc.reshape(1, b * lc, D), ml, ctx_rowf, g1, None, w_in_bf, li, b * lc).reshape(b, lc, N_IN)
        hcf, hcb, hcl = _lru_call(pc, jnp.zeros((b, 2, SUBLANES, LANES), F32), *lru_consts, lc)
        p = _inproj_call(x, ml, lat_row, g1, tab, w_in_bf, li, _seq_tile(l, 1024))
        hf, hb, _ = _lru_call(p, hcl, *lru_consts, _seq_tile(l, 512))
        yb = _attn_call(p, pc, sink_b, _seq_tile(l, 512))
        x = _bout_call(p, hf, hb, yb, x, ml, lat_row, *bout_consts, *outs, li, _seq_tile(l, 512))
        x = _ffn_call(x, ml, lat_row, g2, wi, wo, li, g_fin, li == DEPTH - 1, _seq_tile(l, 512))
        if need_ctx:
            ybc = _cattn_call(pc, sink_b)
            xc = _bout_call(pc, hcf, hcb, ybc, xc, ml, ctx_rowf, *bout_consts, *outs, li, lc)
            xc = _ffn_call(xc.reshape(1, b * lc, D), ml, ctx_rowf, g2, wi, wo, li, g_fin, False,
                           _seq_tile(b * lc, 512)).reshape(b, lc, D)
    return x
```

```python
import functools

import jax
import jax.numpy as jnp
from jax import lax
from jax.experimental import pallas as pl
from jax.experimental.pallas import tpu as pltpu

F32 = jnp.float32
BF16 = jnp.bfloat16

D = 1024
DEPTH = 4
GRID_W = 64
EPS = 1e-6
NEG_INF = -1e30
F32_MIN_NORMAL = 1.17549435e-38
K_A = 3
N_HEADS = 8
N_KV = 2
HEAD_DIM = 128
GROUP = N_HEADS // N_KV
WINDOW = 128
ROPE_THETA = 10000.0
K_C = 4
LRU_BLOCKS = 8
LRU_BS = D // LRU_BLOCKS
LRU_C = 8.0
K_D = 31
D_FF = 2816
N_IN = 12800

C_MA, C_AB, C_MD, C_RG, C_RU, C_GL, C_Q = 0, 2, 3, 5, 6, 7, 11
C_K, C_V = 48, 49
HALO = 16
SUBLANES = 8
LANES = 128
N_CHUNK = D // LANES
SLAB_PAD = 8
VMEM_LIMIT = 56 * 1024 * 1024


def _cparams(sem):
    return pltpu.CompilerParams(dimension_semantics=sem, vmem_limit_bytes=VMEM_LIMIT)


def _resident(shape):
    nd = len(shape)
    return pl.BlockSpec(shape, lambda *_: (0,) * nd, pipeline_mode=pl.Buffered(1))


def _resident_layer(shape, layer):
    nd = len(shape)
    return pl.BlockSpec((None,) + shape, lambda *_: (layer,) + (0,) * nd, pipeline_mode=pl.Buffered(1))


def _modnorm(x, g, shift, scale):
    y = x * lax.rsqrt(jnp.mean(x * x, axis=-1, keepdims=True) + EPS)
    return (y * g) * (1.0 + scale) + shift


def _sigmoid(x):
    return 0.5 * jnp.tanh(0.5 * x) + 0.5


def _mod_kernel(c_ref, w_ref, b_ref, o_ref):
    s = c_ref[...]
    s = s * _sigmoid(s)
    w = w_ref[...]
    s_hi = s.astype(BF16)
    s_lo = (s - s_hi.astype(F32)).astype(BF16)
    w_hi = w.astype(BF16)
    w_lo = (w - w_hi.astype(F32)).astype(BF16)
    acc = jnp.dot(s_hi, w_hi, preferred_element_type=F32)
    acc += jnp.dot(s_lo, w_hi, preferred_element_type=F32)
    acc += jnp.dot(s_hi, w_lo, preferred_element_type=F32)
    o_ref[...] = acc + b_ref[...]


def _mod_call(cc, w_mod, b_mod):
    tn = 1536
    return pl.pallas_call(
        _mod_kernel,
        grid=(DEPTH, 6 * D // tn),
        in_specs=[
            pl.BlockSpec((8, D), lambda l, j: (0, 0)),
            pl.BlockSpec((None, D, tn), lambda l, j: (l, 0, j)),
            pl.BlockSpec((None, 1, tn), lambda l, j: (l, 0, j)),
        ],
        out_specs=pl.BlockSpec((None, 8, tn), lambda l, j: (l, 0, j)),
        out_shape=jax.ShapeDtypeStruct((DEPTH, 8, 6 * D), F32),
        compiler_params=_cparams(("arbitrary", "arbitrary")),
    )(cc, w_mod, b_mod.reshape(DEPTH, 1, 6 * D))


IN_SUB = 512
IN_GROUP = 5
IN_TN = IN_SUB * IN_GROUP
Q_SCALE = HEAD_DIM ** -0.5
IN_PERM = (2, 4, 3, 5, 0, 1, 13, 15, 14, 16, 9, 10, 11, 12, 17, 18, 19, 20, 21, 22, 23, 24, 6, 7, 8)


def _rope(t, tab_ref):
    cos = tab_ref[:, 0:LANES]
    sin_lo = tab_ref[:, LANES:2 * LANES]
    sin_hi = tab_ref[:, 2 * LANES:3 * LANES]
    return t * cos + pltpu.roll(t, 96, 1) * sin_lo + pltpu.roll(t, 32, 1) * sin_hi


def _inproj_kernel(*refs, rope):
    x_ref, mod_ref, g_ref = refs[:3]
    tab_ref = refs[3] if rope else None
    w_refs = refs[-(IN_GROUP + 2):-2]
    o_ref, h_ref = refs[-2:]
    j = pl.program_id(2)

    @pl.when(j == 0)
    def _():
        h_ref[...] = _modnorm(x_ref[...], g_ref[...], mod_ref[0:1, :], mod_ref[1:2, :]).astype(BF16)

    def mm(s):
        return jnp.dot(h_ref[...], w_refs[s][...], preferred_element_type=F32)

    def put(pos, val):
        o_ref[:, pos * IN_SUB:(pos + 1) * IN_SUB] = val.astype(BF16)

    def zero(pos):
        o_ref[:, pos * IN_SUB:(pos + 1) * IN_SUB] = jnp.zeros((o_ref.shape[0], IN_SUB), BF16)

    def rotated(acc, n_rot, scale):
        for c in range(IN_SUB // LANES):
            t = acc[:, c * LANES:(c + 1) * LANES]
            if rope and c < n_rot:
                t = _rope(t, tab_ref)
            if scale != 1.0:
                t = t * scale
            yield c, t

    @pl.when(j == 0)
    def _():
        put(0, mm(0) * mm(1))
        put(1, mm(2) * mm(3))
        zero(2)
        zero(3)
        put(4, mm(4))

    @pl.when(j == 1)
    def _():
        put(0, mm(0))
        put(1, mm(1) * _sigmoid(mm(2)))
        put(2, mm(3) * _sigmoid(mm(4)))
        zero(3)
        zero(4)

    @pl.when(j == 2)
    def _():
        put(0, jax.nn.gelu(mm(0)))
        put(1, jax.nn.gelu(mm(1)))
        put(2, mm(2))
        put(3, mm(3))
        put(4, _sigmoid(mm(4)))

    @pl.when(j == 3)
    def _():
        for s in range(IN_GROUP):
            put(s, _sigmoid(mm(s)))

    @pl.when(j == 4)
    def _():
        put(0, _sigmoid(mm(0)))
        put(1, _sigmoid(mm(1)))
        n_chunk = IN_SUB // LANES
        for s, n_rot, scale in ((2, n_chunk, Q_SCALE), (3, n_chunk, Q_SCALE), (4, n_chunk // 2, 1.0)):
            for c, t in rotated(mm(s), n_rot, scale):
                o_ref[:, s * IN_SUB + c * LANES:s * IN_SUB + (c + 1) * LANES] = t.astype(BF16)


def _w_in_subtile(j, s):
    idx = IN_PERM[s]
    for grp in range(1, N_IN // IN_TN):
        idx = jnp.where(j == grp, IN_PERM[grp * IN_GROUP + s], idx)
    return idx


def _inproj_call(x, mods_l, mod_row, g, tab, w_all, layer, tm):
    bx, lx, _ = x.shape
    rope = tab is not None
    in_specs = [
        pl.BlockSpec((None, tm, D), lambda b, i, j: (b, i, 0)),
        pl.BlockSpec((None, 6, D), lambda b, i, j: (mod_row(b), 0, 0)),
        pl.BlockSpec((1, D), lambda b, i, j: (0, 0)),
    ]
    args = [x, mods_l, g]
    if rope:
        in_specs.append(pl.BlockSpec((tm, 3 * LANES), lambda b, i, j: (i, 0)))
        args.append(tab)
    for s in range(IN_GROUP):
        in_specs.append(pl.BlockSpec((None, D, IN_SUB), lambda b, i, j, s=s: (layer, 0, _w_in_subtile(j, s))))
        args.append(w_all)
    return pl.pallas_call(
        functools.partial(_inproj_kernel, rope=rope),
        grid=(bx, lx // tm, N_IN // IN_TN),
        in_specs=in_specs,
        out_specs=pl.BlockSpec((None, tm, IN_TN), lambda b, i, j: (b, i, j)),
        out_shape=jax.ShapeDtypeStruct((bx, lx, N_IN), BF16),
        scratch_shapes=[pltpu.VMEM((tm, D), BF16)],
        compiler_params=_cparams(("arbitrary", "arbitrary", "arbitrary")),
    )(*args)


def _pcol(t, c, width=D):
    return pl.BlockSpec((None, t, width), lambda b, i: (b, i, c))


def _halo_specs(t, n_t, c, tile_of=lambda i: i):
    per = t // HALO
    last = n_t * per - 1
    return [
        pl.BlockSpec((None, HALO, D), lambda b, i: (b, jnp.maximum(tile_of(i) * per - 1, 0), c)),
        pl.BlockSpec((None, t, D), lambda b, i: (b, tile_of(i), c)),
        pl.BlockSpec((None, HALO, D), lambda b, i: (b, jnp.minimum((tile_of(i) + 1) * per, last), c)),
    ]


def _edge_masks(tile, n_t):
    has_prev = jnp.where(tile > 0, 1.0, 0.0).astype(F32)
    has_next = jnp.where(tile < n_t - 1, 1.0, 0.0).astype(F32)
    return has_prev, has_next


def _tile_rows(j, pitch):
    return (pl.ds(j, N_CHUNK, stride=pitch), slice(None))


def _fill_ext_slab(slab, t, prev, cur, nxt):
    pitch = t + 2 * HALO + SLAB_PAD
    for c in range(N_CHUNK):
        cs = slice(c * LANES, (c + 1) * LANES)
        slab[c * pitch:c * pitch + HALO, :] = prev(cs)
        slab[c * pitch + HALO:c * pitch + HALO + t, :] = cur(cs)
        slab[c * pitch + HALO + t:c * pitch + 2 * HALO + t, :] = nxt(cs)


def _loop(n, body, inline, unroll=1):
    if inline:
        for i in range(n):
            body(i, 0)
    else:
        lax.fori_loop(0, n, body, 0, unroll=unroll)


def _fold(slab, pitch, dst, n_rows, inline=False):
    def body(r, carry):
        dst[r] = slab[_tile_rows(r, pitch)]
        return carry

    _loop(n_rows, body, inline, unroll=8)


def _conv_folded(ext_f, w_ref, k_taps, t, emit, rows_per_iter=4, inline=False):
    off = HALO - k_taps // 2

    def body(it, carry):
        t0 = it * rows_per_iter
        loaded = {}

        def x(r):
            if r not in loaded:
                loaded[r] = ext_f[t0 + (off + r)]
            return loaded[r]

        n_part = min(4, k_taps)
        parts = [[None] * n_part for _ in range(rows_per_iter)]
        for k in range(k_taps):
            w = w_ref[k]
            for r in range(rows_per_iter):
                term = w * x(k + r)
                prev = parts[r][k % n_part]
                parts[r][k % n_part] = term if prev is None else prev + term
        for r in range(rows_per_iter):
            acc = parts[r]
            while len(acc) > 1:
                acc = [acc[i] + acc[i + 1] for i in range(0, len(acc) - 1, 2)] + ([acc[-1]] if len(acc) % 2 else [])
            emit(t0 + r, acc[0])
        return carry

    _loop(t // rows_per_iter, body, inline)


SCAN_GROUP = 4


def _lru_kernel(pfp, pfc, pfn, pbp, pbc, pbn, cw_ref, cb_ref, wab_ref, bab_ref, lam_ref, h0_ref,
                hf_ref, hb_ref, hl_ref, slab_e0, ext_f0, slab_u0, slab_e1, ext_f1, slab_u1, af, bf, ab, bb, carry,
                *, t, n_t):
    i = pl.program_id(1)
    hf_slab, hb_slab = slab_u0, slab_u1
    pitch_e = t + 2 * HALO + SLAB_PAD
    pitch = t + SLAB_PAD

    @pl.when(i == 0)
    def _():
        carry[...] = h0_ref[...]

    for d, (prv, cur, nxt, a_slab, b_slab, slab_e, ext_f, slab_u) in enumerate((
            (pfp, pfc, pfn, af, bf, slab_e0, ext_f0, slab_u0), (pbp, pbc, pbn, ab, bb, slab_e1, ext_f1, slab_u1))):
        tile = i if d == 0 else n_t - 1 - i
        has_prev, has_next = _edge_masks(tile, n_t)
        _fill_ext_slab(slab_e, t,
                       lambda cs: prv[:, cs].astype(F32) * has_prev,
                       lambda cs: cur[:, cs].astype(F32),
                       lambda cs: nxt[:, cs].astype(F32) * has_next)
        _fold(slab_e, pitch_e, ext_f, t + 2 * HALO, inline=True)

        bias = cb_ref[...]

        def emit(row, tile_val):
            slab_u[_tile_rows(row, pitch)] = tile_val + bias

        _conv_folded(ext_f, cw_ref, K_C, t, emit, inline=True)

        neg_lam = -lam_ref[d:d + 1, :]
        softplus = jnp.maximum(neg_lam, 0.0) + jnp.log1p(jnp.exp(-jnp.abs(neg_lam)))
        half_c8 = (-0.5 * LRU_C) * softplus
        for n in range(LRU_BLOCKS):
            cs = slice(n * LRU_BS, (n + 1) * LRU_BS)
            rs = slice(n * pitch, n * pitch + t)
            u = slab_u[rs, :]
            g = jnp.dot(u.astype(BF16), wab_ref[d, n], preferred_element_type=F32)
            t_r = jnp.tanh(g[:, :LRU_BS] + bab_ref[d:d + 1, cs])
            t_i = jnp.tanh(g[:, LRU_BS:] + bab_ref[2 + d:3 + d, cs])
            log_a = half_c8[:, cs] * t_r + half_c8[:, cs]
            a = jnp.exp(log_a)
            y = jnp.tanh(log_a) * (-1.0 - a * a)
            gain = y * lax.rsqrt(jnp.maximum(y, F32_MIN_NORMAL))
            a_slab[rs, :] = a
            b_slab[rs, :] = gain * ((0.5 * t_i + 0.5) * u)

    def chain(a_slab, b_slab, h_slab, h, positions):
        pa = pb = out = None
        for pos in positions:
            rows = _tile_rows(pos, pitch)
            a = a_slab[rows]
            b = b_slab[rows]
            if pa is None:
                pa, pb = a, b
            else:
                pa, pb = a * pa, a * pb + b
            out = pa * h + pb
            h_slab[rows] = out
        return out

    def step(it, c):
        h_f, h_b = c
        j0 = it * SCAN_GROUP
        h_f = chain(af, bf, hf_slab, h_f, [j0 + g for g in range(SCAN_GROUP)])
        h_b = chain(ab, bb, hb_slab, h_b, [t - 1 - j0 - g for g in range(SCAN_GROUP)])
        return h_f, h_b

    h_f, h_b = lax.fori_loop(0, t // SCAN_GROUP, step, (carry[0], carry[1]), unroll=2)
    carry[0] = h_f
    carry[1] = h_b
    hl_ref[...] = carry[...]
    for n in range(N_CHUNK):
        cs = slice(n * LANES, (n + 1) * LANES)
        rs = slice(n * pitch, n * pitch + t)
        hf_ref[:, cs] = hf_slab[rs, :].astype(BF16)
        hb_ref[:, cs] = hb_slab[rs, :].astype(BF16)


def _lru_call(p, h0, conv_c, conv_c_b, wab, bab, lam, t):
    b, l, _ = p.shape
    n_t = l // t
    rev = lambda i: n_t - 1 - i
    small = lambda shape: pl.BlockSpec(shape, lambda b, i: (0,) * len(shape))
    state = lambda: pl.BlockSpec((None, 2, SUBLANES, LANES), lambda b, i: (b, 0, 0, 0))
    slab = lambda rows: pltpu.VMEM((N_CHUNK * (rows + SLAB_PAD), LANES), F32)
    return pl.pallas_call(
        functools.partial(_lru_kernel, t=t, n_t=n_t),
        grid=(b, n_t),
        in_specs=(_halo_specs(t, n_t, C_RU) + _halo_specs(t, n_t, C_RU, rev) + [
            small((K_C, SUBLANES, LANES)), small((SUBLANES, LANES)),
            small((2, LRU_BLOCKS, LRU_BS, 2 * LRU_BS)), small((4, D)), small((2, D)), state()]),
        out_specs=[
            pl.BlockSpec((None, t, D), lambda b, i: (b, i, 0)),
            pl.BlockSpec((None, t, D), lambda b, i: (b, rev(i), 0)),
            state(),
        ],
        out_shape=[
            jax.ShapeDtypeStruct((b, l, D), BF16),
            jax.ShapeDtypeStruct((b, l, D), BF16),
            jax.ShapeDtypeStruct((b, 2, SUBLANES, LANES), F32),
        ],
        scratch_shapes=[
            slab(t + 2 * HALO), pltpu.VMEM((t + 2 * HALO, SUBLANES, LANES), F32), slab(t),
            slab(t + 2 * HALO), pltpu.VMEM((t + 2 * HALO, SUBLANES, LANES), F32), slab(t),
            slab(t), slab(t), slab(t), slab(t), pltpu.VMEM((2, SUBLANES, LANES), F32),
        ],
        compiler_params=_cparams(("arbitrary", "arbitrary")),
    )(p, p, p, p, p, p, conv_c.reshape(K_C, SUBLANES, LANES), conv_c_b.reshape(SUBLANES, LANES), wab, bab, lam, h0)


def _sink_column(sink_ref, kh, rows_per_head):
    rid = lax.broadcasted_iota(jnp.int32, (GROUP * rows_per_head, 1), 0)
    col = sink_ref[kh * GROUP + GROUP - 1:kh * GROUP + GROUP, 0:1]
    for g in range(GROUP - 2, -1, -1):
        col = jnp.where(rid < (g + 1) * rows_per_head, sink_ref[kh * GROUP + g:kh * GROUP + g + 1, 0:1], col)
    return col


def _qk(q, k):
    return lax.dot_general(q, k, (((1,), (1,)), ((), ())), preferred_element_type=F32)


def _stack_heads(q_ref, rs, kh):
    return jnp.concatenate(
        [q_ref[rs, (kh * GROUP + g) * HEAD_DIM:(kh * GROUP + g + 1) * HEAD_DIM] for g in range(GROUP)], axis=0)


def _attn_kernel(q_ref, k_ref, v_ref, kc_ref, vc_ref, sink_ref, o_ref, kpad, vpad, vcx, *, tq, l):
    i = pl.program_id(1)
    blk = WINDOW
    nqb = tq // blk
    nb = l // blk
    rows = GROUP * blk
    vw = 2 * HEAD_DIM

    @pl.when(i == 0)
    def _():
        kpad[0:blk, :] = jnp.zeros((blk, N_KV * HEAD_DIM), BF16)
        kpad[blk:blk + l, :] = k_ref[...]
        kpad[blk + l:2 * blk + l, :] = jnp.zeros((blk, N_KV * HEAD_DIM), BF16)
        vpad[0:blk, :] = jnp.zeros((blk, N_KV * vw), BF16)
        vpad[blk + l:2 * blk + l, :] = jnp.zeros((blk, N_KV * vw), BF16)
        for kh in range(N_KV):
            vpad[blk:blk + l, kh * vw:kh * vw + HEAD_DIM] = v_ref[:, kh * HEAD_DIM:(kh + 1) * HEAD_DIM]
            vpad[blk:blk + l, kh * vw + HEAD_DIM:(kh + 1) * vw] = jnp.ones((l, HEAD_DIM), BF16)
            vcx[:, kh * vw:kh * vw + HEAD_DIM] = vc_ref[:, kh * HEAD_DIM:(kh + 1) * HEAD_DIM]
            vcx[:, kh * vw + HEAD_DIM:(kh + 1) * vw] = jnp.ones((vc_ref.shape[0], HEAD_DIM), BF16)

    qrow = lax.broadcasted_iota(jnp.int32, (rows, blk), 0) & (blk - 1)
    kcol = lax.broadcasted_iota(jnp.int32, (rows, blk), 1)
    bias_prev = jnp.where(kcol >= qrow, 0.0, NEG_INF).astype(F32)
    bias_next = jnp.where(kcol <= qrow, 0.0, NEG_INF).astype(F32)

    units = [(qb, kh) for qb in range(nqb) for kh in range(N_KV)]

    def scores(qb, kh):
        n = i * nqb + qb
        start = pl.multiple_of(n * blk, blk)
        ks = slice(kh * HEAD_DIM, (kh + 1) * HEAD_DIM)
        q4 = _stack_heads(q_ref, slice(qb * blk, (qb + 1) * blk), kh)
        s_loc = _qk(q4, kpad[pl.ds(start, 3 * blk), ks])
        s_ctx = _qk(q4, kc_ref[:, ks])
        no_prev = jnp.where(n > 0, 0.0, NEG_INF).astype(F32)
        no_next = jnp.where(n < nb - 1, 0.0, NEG_INF).astype(F32)
        return (s_loc[:, 0:blk] + (bias_prev + no_prev), s_loc[:, blk:2 * blk],
                s_loc[:, 2 * blk:3 * blk] + (bias_next + no_next), s_ctx)

    def finish(qb, kh, s_prev, s_own, s_next, s_ctx):
        n = i * nqb + qb
        start = pl.multiple_of(n * blk, blk)
        s_sink = _sink_column(sink_ref, kh, blk)
        m_el = jnp.maximum(jnp.maximum(s_prev, s_own), s_next)
        for c in range(s_ctx.shape[1] // blk):
            m_el = jnp.maximum(m_el, s_ctx[:, c * blk:(c + 1) * blk])
        m = jnp.maximum(jnp.max(m_el, axis=-1, keepdims=True), s_sink)
        p_loc = jnp.concatenate([jnp.exp(s_prev - m).astype(BF16), jnp.exp(s_own - m).astype(BF16),
                                 jnp.exp(s_next - m).astype(BF16)], axis=1)
        p_ctx = jnp.exp(s_ctx - m).astype(BF16)
        vs = slice(kh * vw, (kh + 1) * vw)
        o = jnp.dot(p_loc, vpad[pl.ds(start, 3 * blk), vs], preferred_element_type=F32)
        o += jnp.dot(p_ctx, vcx[:, vs], preferred_element_type=F32)
        den = o[:, HEAD_DIM:vw] + jnp.exp(s_sink - m)
        o = o[:, 0:HEAD_DIM] / den
        for g in range(GROUP):
            hs = slice((kh * GROUP + g) * HEAD_DIM, (kh * GROUP + g + 1) * HEAD_DIM)
            o_ref[qb * blk:(qb + 1) * blk, hs] = o[g * blk:(g + 1) * blk, :].astype(BF16)

    pending = scores(*units[0])
    for n_u, unit in enumerate(units):
        nxt = scores(*units[n_u + 1]) if n_u + 1 < len(units) else None
        finish(*unit, *pending)
        pending = nxt


def _attn_call(p, pc, sink_b, tq):
    b, l, _ = p.shape
    lc = pc.shape[1]
    kvw = N_KV * HEAD_DIM
    return pl.pallas_call(
        functools.partial(_attn_kernel, tq=tq, l=l),
        grid=(b, l // tq),
        in_specs=[
            _pcol(tq, C_Q),
            pl.BlockSpec((None, l, kvw), lambda b, i: (b, 0, C_K)),
            pl.BlockSpec((None, l, kvw), lambda b, i: (b, 0, C_V)),
            pl.BlockSpec((None, lc, kvw), lambda b, i: (b, 0, C_K)),
            pl.BlockSpec((None, lc, kvw), lambda b, i: (b, 0, C_V)),
            pl.BlockSpec((N_HEADS, LANES), lambda b, i: (0, 0)),
        ],
        out_specs=pl.BlockSpec((None, tq, D), lambda b, i: (b, i, 0)),
        out_shape=jax.ShapeDtypeStruct((b, l, D), BF16),
        scratch_shapes=[pltpu.VMEM((l + 2 * WINDOW, kvw), BF16), pltpu.VMEM((l + 2 * WINDOW, 2 * kvw), BF16),
                        pltpu.VMEM((lc, 2 * kvw), BF16)],
        compiler_params=_cparams(("arbitrary", "arbitrary")),
    )(p, p, p, pc, pc, sink_b)


def _cattn_kernel(q_ref, kc_ref, vc_ref, sink_ref, o_ref, *, lc):
    for kh in range(N_KV):
        ks = slice(kh * HEAD_DIM, (kh + 1) * HEAD_DIM)
        q4 = _stack_heads(q_ref, slice(0, lc), kh)
        s = _qk(q4, kc_ref[:, ks])
        s_sink = _sink_column(sink_ref, kh, lc)
        m = jnp.maximum(jnp.max(s, axis=-1, keepdims=True), s_sink)
        p = jnp.exp(s - m)
        den = jnp.sum(p, axis=-1, keepdims=True) + jnp.exp(s_sink - m)
        o = jnp.dot(p.astype(BF16), vc_ref[:, ks], preferred_element_type=F32) / den
        for g in range(GROUP):
            hs = slice((kh * GROUP + g) * HEAD_DIM, (kh * GROUP + g + 1) * HEAD_DIM)
            o_ref[:, hs] = o[g * lc:(g + 1) * lc, :].astype(BF16)


def _cattn_call(pc, sink_b):
    b, lc, _ = pc.shape
    kvw = N_KV * HEAD_DIM
    return pl.pallas_call(
        functools.partial(_cattn_kernel, lc=lc),
        grid=(b,),
        in_specs=[
            pl.BlockSpec((None, lc, D), lambda b: (b, 0, C_Q)),
            pl.BlockSpec((None, lc, kvw), lambda b: (b, 0, C_K)),
            pl.BlockSpec((None, lc, kvw), lambda b: (b, 0, C_V)),
            pl.BlockSpec((N_HEADS, LANES), lambda b: (0, 0)),
        ],
        out_specs=pl.BlockSpec((None, lc, D), lambda b: (b, 0, 0)),
        out_shape=jax.ShapeDtypeStruct((b, lc, D), BF16),
        compiler_params=_cparams(("arbitrary",)),
    )(pc, pc, pc, sink_b)


def _bout_kernel(ab_ref, map_, mac, man, rg_ref, mdp, mdc, mdn, ga_ref, gb_ref, gc_ref, gd_ref,
                 hf_ref, hb_ref, yb_ref, x_ref, mod_ref,
                 ca_ref, cd_ref, cdb_ref, lng_ref, lnb_ref, wa_ref, wb_ref, wc_ref, wd_ref, wo_ref,
                 o_ref, slab_e, ext_f, slab_z, z_scr, *, t, n_t):
    i = pl.program_id(1)
    has_prev, has_next = _edge_masks(i, n_t)
    pitch_e = t + 2 * HALO + SLAB_PAD
    pitch = t + SLAB_PAD

    def f32(ref):
        return ref[...].astype(F32)

    def conv(k_taps, w_ref, prv, cur, nxt):
        _fill_ext_slab(slab_e, t,
                       lambda cs: prv[:, cs].astype(F32) * has_prev,
                       lambda cs: cur[:, cs].astype(F32),
                       lambda cs: nxt[:, cs].astype(F32) * has_next)
        _fold(slab_e, pitch_e, ext_f, t + 2 * HALO, inline=True)

        def emit(row, tile_val):
            slab_z[_tile_rows(row, pitch)] = tile_val

        _conv_folded(ext_f, w_ref, k_taps, t, emit, inline=True)
        for c in range(N_CHUNK):
            z_scr[:, c * LANES:(c + 1) * LANES] = slab_z[c * pitch:c * pitch + t, :]

    def gated(gate_ref, pre, w_ref):
        return f32(gate_ref) * jnp.dot(pre.astype(BF16), w_ref[...], preferred_element_type=F32)

    conv(K_A, ca_ref, map_, mac, man)
    merged = gated(ga_ref, f32(ab_ref) * z_scr[...], wa_ref)
    merged += gated(gb_ref, yb_ref[...], wb_ref)
    merged += gated(gc_ref, f32(rg_ref) * (f32(hf_ref) + f32(hb_ref)), wc_ref)
    conv(K_D, cd_ref, mdp, mdc, mdn)
    z = z_scr[...] + cdb_ref[...]
    mu = jnp.mean(z, axis=-1, keepdims=True)
    zc = z - mu
    var = jnp.mean(zc * zc, axis=-1, keepdims=True)
    y = zc * lax.rsqrt(var + EPS) * lng_ref[...] + lnb_ref[...]
    merged += gated(gd_ref, y * _sigmoid(y), wd_ref)

    o = jnp.dot(merged.astype(BF16), wo_ref[...], preferred_element_type=F32)
    o_ref[...] = x_ref[...] + mod_ref[2:3, :] * o


def _bout_call(p, hf, hb, yb, x, mods_l, mod_row, conv_a, conv_d, conv_d_b, ln_g, ln_b, wa, wb, wc, wd, wo, layer,
               t):
    b, l, _ = p.shape
    n_t = l // t
    row = lambda: pl.BlockSpec((None, t, D), lambda b, i: (b, i, 0))
    p_specs = ([_pcol(t, C_AB)] + _halo_specs(t, n_t, C_MA) + [_pcol(t, C_RG)] + _halo_specs(t, n_t, C_MD)
               + [_pcol(t, C_GL + k) for k in range(4)])
    in_specs = (p_specs
                + [row(), row(), row(), row(),
                   pl.BlockSpec((None, 6, D), lambda b, i: (mod_row(b), 0, 0)),
                   _resident((K_A, SUBLANES, LANES)), _resident((K_D, SUBLANES, LANES)),
                   _resident((1, D)), _resident((1, D)), _resident((1, D))]
                + [_resident_layer((D, D), layer) for _ in range(5)])
    slab = lambda rows: pltpu.VMEM((N_CHUNK * (rows + SLAB_PAD), LANES), F32)
    return pl.pallas_call(
        functools.partial(_bout_kernel, t=t, n_t=n_t),
        grid=(b, n_t),
        in_specs=in_specs,
        out_specs=pl.BlockSpec((None, t, D), lambda b, i: (b, i, 0)),
        out_shape=jax.ShapeDtypeStruct((b, l, D), F32),
        scratch_shapes=[slab(t + 2 * HALO), pltpu.VMEM((t + 2 * HALO, SUBLANES, LANES), F32), slab(t),
                        pltpu.VMEM((t, D), F32)],
        compiler_params=_cparams(("arbitrary", "arbitrary")),
    )(*([p] * len(p_specs)), hf, hb, yb, x, mods_l, conv_a.reshape(K_A, SUBLANES, LANES),
      conv_d.reshape(K_D, SUBLANES, LANES), conv_d_b, ln_g, ln_b, wa, wb, wc, wd, wo)


FF_CHUNK = 256


def _ffn_kernel(x_ref, mod_ref, g_ref, wi_ref, wo_ref, gf_ref, o_ref, acc, *, final):
    x = x_ref[...]
    h = _modnorm(x, g_ref[...], mod_ref[3:4, :], mod_ref[4:5, :]).astype(BF16)
    for c in range(D_FF // FF_CHUNK):
        gate = jnp.dot(h, wi_ref[:, c * FF_CHUNK:(c + 1) * FF_CHUNK], preferred_element_type=F32)
        up = jnp.dot(h, wi_ref[:, D_FF + c * FF_CHUNK:D_FF + (c + 1) * FF_CHUNK], preferred_element_type=F32)
        act = (gate * _sigmoid(gate) * up).astype(BF16)
        part = jnp.dot(act, wo_ref[c * FF_CHUNK:(c + 1) * FF_CHUNK, :], preferred_element_type=F32)
        if c == 0:
            acc[...] = part
        else:
            acc[...] += part
    y = x + mod_ref[5:6, :] * acc[...]
    if final:
        y = y * lax.rsqrt(jnp.mean(y * y, axis=-1, keepdims=True) + EPS) * gf_ref[...]
    o_ref[...] = y


def _ffn_call(x, mods_l, mod_row, g, wi, wo, layer, g_final, final, t):
    b, l, _ = x.shape
    return pl.pallas_call(
        functools.partial(_ffn_kernel, final=final),
        grid=(b, l // t),
        in_specs=[
            pl.BlockSpec((None, t, D), lambda b, i: (b, i, 0)),
            pl.BlockSpec((None, 6, D), lambda b, i: (mod_row(b), 0, 0)),
            _resident((1, D)), _resident_layer((D, 2 * D_FF), layer), _resident_layer((D_FF, D), layer),
            _resident((1, D)),
        ],
        out_specs=pl.BlockSpec((None, t, D), lambda b, i: (b, i, 0)),
        out_shape=jax.ShapeDtypeStruct((b, l, D), F32),
        scratch_shapes=[pltpu.VMEM((t, D), F32)],
        compiler_params=_cparams(("arbitrary", "arbitrary")),
    )(x, mods_l, g, wi, wo, g_final)


def _rope_table(n_tok):
    rows = n_tok // GRID_W
    half = HEAD_DIM // 2
    inv_freq = ROPE_THETA ** (-jnp.arange(0, half, 2, dtype=F32) / half)
    ang_r = jnp.arange(rows, dtype=F32)[:, None] * inv_freq[None, :]
    ang_c = jnp.arange(GRID_W, dtype=F32)[:, None] * inv_freq[None, :]

    def table(f):
        r = jnp.broadcast_to(f(ang_r)[:, None, :], (rows, GRID_W, half // 2))
        c = jnp.broadcast_to(f(ang_c)[None, :, :], (rows, GRID_W, half // 2))
        return jnp.concatenate([r, r, c, c], axis=-1).reshape(n_tok, HEAD_DIM)

    cos, sin = table(jnp.cos), table(jnp.sin)
    first = (jnp.arange(HEAD_DIM) % half) < half // 2
    return jnp.concatenate([cos, jnp.where(first, -sin, 0.0), jnp.where(first, 0.0, sin)], axis=-1)


def _seq_tile(l, want):
    return min(l, want)


def kernel(x, c, ctx, c_ctx, w_mod, b_mod, g_mix, w_in, conv_a, sink, conv_c, conv_c_b, lru_wa, lru_ba, lru_wx,
           lru_bx, lru_lam, conv_d, conv_d_b, ln_d_g, ln_d_b, w_a_out, w_b_out, w_c_out, w_d_out, w_out, g_ffn,
           w_ffn_in, w_ffn_out, g_final):
    b, l, _ = x.shape
    lc = ctx.shape[1]
    ctx_row = b
    cc = jnp.zeros((8, D), F32).at[:b].set(c).at[ctx_row].set(c_ctx)
    mods = _mod_call(cc, w_mod, b_mod).reshape(DEPTH, 8, 6, D)
    tab = _rope_table(l)
    lat_row = lambda bi: bi
    ctx_rowf = lambda bi: ctx_row
    g_fin = g_final.reshape(1, D)

    w_in_bf = w_in.astype(BF16)
    outs = [w.astype(BF16) for w in (w_a_out, w_b_out, w_c_out, w_d_out, w_out)]
    wi = w_ffn_in.astype(BF16)
    wo = w_ffn_out.astype(BF16)
    xc = ctx
    for li in range(DEPTH):
        need_ctx = li < DEPTH - 1
        wab = (0.5 * jnp.concatenate([lru_wa[li], lru_wx[li]], axis=-1)).astype(BF16)
        bab = 0.5 * jnp.concatenate([lru_ba[li], lru_bx[li]], axis=0)
        sink_b = jnp.broadcast_to(sink[li][:, None], (N_HEADS, LANES))
        g1 = g_mix[li].reshape(1, D)
        g2 = g_ffn[li].reshape(1, D)
        bout_consts = (conv_a[li], conv_d[li], conv_d_b[li].reshape(1, D), ln_d_g[li].reshape(1, D),
                       ln_d_b[li].reshape(1, D))
        ml = mods[li]
        lru_consts = (conv_c[li], conv_c_b[li], wab, bab, lru_lam[li])

        pc = _inproj_call(xc.reshape(1, b * lc, D), ml, ctx_rowf, g1, None, w_in_bf, li, b * lc).reshape(b, lc, N_IN)
        hcf, hcb, hcl = _lru_call(pc, jnp.zeros((b, 2, SUBLANES, LANES), F32), *lru_consts, lc)
        p = _inproj_call(x, ml, lat_row, g1, tab, w_in_bf, li, _seq_tile(l, 1024))
        hf, hb, _ = _lru_call(p, hcl, *lru_consts, _seq_tile(l, 512))
        yb = _attn_call(p, pc, sink_b, _seq_tile(l, 512))
        x = _bout_call(p, hf, hb, yb, x, ml, lat_row, *bout_consts, *outs, li, _seq_tile(l, 512))
        x = _ffn_call(x, ml, lat_row, g2, wi, wo, li, g_fin, li == DEPTH - 1, _seq_tile(l, 1024))
        if need_ctx:
            ybc = _cattn_call(pc, sink_b)
            xc = _bout_call(pc, hcf, hcb, ybc, xc, ml, ctx_rowf, *bout_consts, *outs, li, lc)
            xc = _ffn_call(xc.reshape(1, b * lc, D), ml, ctx_rowf, g2, wi, wo, li, g_fin, False,
                           _seq_tile(b * lc, 512)).reshape(b, lc, D)
    return x
```
